```python
import jax, jax.numpy as jnp
from jax import lax
import numpy as np

D_MODEL = 1024
BATCH = 16
SEQ = 2048
DEPTH = 1

D_MIX = D_MODEL
D_CONV = D_MIX // 2
CONV_HEADS = 8
CONV_WIDTH = 31
D_POOL = D_MIX - D_CONV
POOL_WINDOWS = (2, 4, 8, 16)
POOL_GROUPS = len(POOL_WINDOWS)
POOL_GROUP_DIM = D_POOL // POOL_GROUPS
D_IN = 2 * D_CONV + D_POOL
N_MEM = 256
XATTN_HEADS = 4
XATTN_HEAD_DIM = D_MODEL // XATTN_HEADS
D_FF = 2816
FFN_CONV_WIDTH = 3
EPS = 1e-6

kernel_name = "hybrid_conformer_pool_xattn_convffn"


def rmsnorm(x, g):
    xf = x.astype(jnp.float32)
    y = xf * lax.rsqrt(jnp.mean(xf * xf, axis=-1, keepdims=True) + EPS)
    return (y * g.astype(jnp.float32)).astype(x.dtype)


def layernorm(x, g, b):
    xf = x.astype(jnp.float32)
    mu = jnp.mean(xf, axis=-1, keepdims=True)
    var = jnp.mean(jnp.square(xf - mu), axis=-1, keepdims=True)
    y = (xf - mu) * lax.rsqrt(var + EPS)
    return (y * g.astype(jnp.float32) + b.astype(jnp.float32)).astype(x.dtype)


def causal_depthwise_conv(x, w, b):
    k, c = w.shape
    y = lax.conv_general_dilated(
        x, w[:, None, :].astype(x.dtype), window_strides=(1,),
        padding=[(k - 1, 0)], dimension_numbers=("NWC", "WIO", "NWC"),
        feature_group_count=c)
    return y + b.astype(x.dtype)


def conformer_conv_mixer(u, dw_w, dw_b, ln_g, ln_b):
    val, gate = jnp.split(u, 2, axis=-1)
    h = val * jax.nn.sigmoid(gate)
    h = causal_depthwise_conv(h, dw_w, dw_b)
    h = layernorm(h, ln_g, ln_b)
    return jax.nn.silu(h)


def causal_window_mean_minus_self(v, w):
    s = v.shape[1]
    vf = v.astype(jnp.float32)
    c = jnp.cumsum(vf, axis=1)
    c_shift = jnp.pad(c, ((0, 0), (w, 0), (0, 0)))[:, :s]
    count = jnp.minimum(jnp.arange(1, s + 1, dtype=jnp.float32), float(w))
    mean = (c - c_shift) / count[None, :, None]
    return (mean - vf).astype(v.dtype)


def pooling_mixer(u, pool_w, pool_scale):
    groups = jnp.split(u, POOL_GROUPS, axis=-1)
    pooled = jnp.stack([causal_window_mean_minus_self(gv, w)
                        for gv, w in zip(groups, POOL_WINDOWS)], axis=2)
    mixed = jnp.einsum("bsgc,gcd->bsgd", pooled, pool_w.astype(u.dtype))
    b, s = u.shape[:2]
    return mixed.reshape(b, s, D_POOL) * pool_scale.astype(u.dtype)


def memory_cross_attention(h, mem_n, w_q, w_kv, w_o):
    b, s, _ = h.shape
    q = (h @ w_q).reshape(b, s, XATTN_HEADS, XATTN_HEAD_DIM)
    k, v = jnp.split(mem_n @ w_kv, 2, axis=-1)
    k = k.reshape(b, N_MEM, XATTN_HEADS, XATTN_HEAD_DIM)
    v = v.reshape(b, N_MEM, XATTN_HEADS, XATTN_HEAD_DIM)
    scores = jnp.einsum("bqhd,bkhd->bhqk", q.astype(jnp.float32), k.astype(jnp.float32))
    probs = jax.nn.softmax(scores * (XATTN_HEAD_DIM ** -0.5), axis=-1).astype(h.dtype)
    o = jnp.einsum("bhqk,bkhd->bqhd", probs, v).reshape(b, s, D_MODEL)
    return o @ w_o


def conv_ffn(h, w_up, dw_w, dw_b, w_down):
    u = causal_depthwise_conv(h @ w_up, dw_w, dw_b)
    gate, val = jnp.split(u, 2, axis=-1)
    return (jax.nn.silu(gate) * val) @ w_down


def setup_inputs(seed: int = 0) -> dict:
    key = jax.random.key(seed)
    ks = jax.random.split(key, 24)
    f32 = jnp.float32

    def nrm(k, shape, scale):
        return jax.random.normal(k, shape, f32) * scale

    def gain(k, shape):
        return 1.0 + 0.05 * jax.random.normal(k, shape, f32)

    L = DEPTH
    return {
        "x": jax.random.normal(ks[0], (BATCH, SEQ, D_MODEL), f32),
        "mem": jax.random.normal(ks[1], (BATCH, N_MEM, D_MODEL), f32),
        "norm_mix_g": gain(ks[2], (L, D_MODEL)),
        "w_in": nrm(ks[3], (L, D_MODEL, D_IN), D_MODEL ** -0.5),
        "conv_dw_w": nrm(ks[4], (L, CONV_WIDTH, D_CONV), CONV_WIDTH ** -0.5),
        "conv_dw_b": nrm(ks[5], (L, D_CONV), 0.02),
        "conv_ln_g": gain(ks[6], (L, D_CONV)),
        "conv_ln_b": nrm(ks[7], (L, D_CONV), 0.02),
        "pool_w": nrm(ks[8], (L, POOL_GROUPS, POOL_GROUP_DIM, POOL_GROUP_DIM), POOL_GROUP_DIM ** -0.5),
        "pool_scale": gain(ks[9], (L, D_POOL)),
        "w_out": nrm(ks[10], (L, D_MIX, D_MODEL), D_MIX ** -0.5),
        "norm_xattn_g": gain(ks[11], (L, D_MODEL)),
        "norm_mem_g": gain(ks[12], (L, D_MODEL)),
        "w_q": nrm(ks[13], (L, D_MODEL, D_MODEL), D_MODEL ** -0.5),
        "w_kv": nrm(ks[14], (L, D_MODEL, 2 * D_MODEL), D_MODEL ** -0.5),
        "w_o": nrm(ks[15], (L, D_MODEL, D_MODEL), D_MODEL ** -0.5),
        "norm_ffn_g": gain(ks[16], (L, D_MODEL)),
        "w_up": nrm(ks[17], (L, D_MODEL, 2 * D_FF), D_MODEL ** -0.5),
        "ffn_dw_w": nrm(ks[18], (L, FFN_CONV_WIDTH, 2 * D_FF), FFN_CONV_WIDTH ** -0.5),
        "ffn_dw_b": nrm(ks[19], (L, 2 * D_FF), 0.02),
        "w_down": nrm(ks[20], (L, D_FF, D_MODEL), D_FF ** -0.5),
        "norm_final_g": gain(ks[21], (D_MODEL,)),
    }


def reference(x, mem, norm_mix_g, w_in, conv_dw_w, conv_dw_b, conv_ln_g, conv_ln_b,
              pool_w, pool_scale, w_out, norm_xattn_g, norm_mem_g, w_q, w_kv, w_o,
              norm_ffn_g, w_up, ffn_dw_w, ffn_dw_b, w_down, norm_final_g):
    for l in range(DEPTH):
        h = rmsnorm(x, norm_mix_g[l])
        u = h @ w_in[l]
        u_conv = u[..., :2 * D_CONV]
        u_pool = u[..., 2 * D_CONV:]
        y_conv = conformer_conv_mixer(u_conv, conv_dw_w[l], conv_dw_b[l],
                                      conv_ln_g[l], conv_ln_b[l])
        y_pool = pooling_mixer(u_pool, pool_w[l], pool_scale[l])
        y = jnp.concatenate([y_conv, y_pool], axis=-1)
        x = x + y @ w_out[l]
        h = rmsnorm(x, norm_xattn_g[l])
        mem_n = rmsnorm(mem, norm_mem_g[l])
        x = x + memory_cross_attention(h, mem_n, w_q[l], w_kv[l], w_o[l])
        h = rmsnorm(x, norm_ffn_g[l])
        x = x + conv_ffn(h, w_up[l], ffn_dw_w[l], ffn_dw_b[l], w_down[l])
    return rmsnorm(x, norm_final_g)
```

```python
import functools

import jax
import jax.numpy as jnp
from jax import lax
from jax.experimental import pallas as pl
from jax.experimental.pallas import tpu as pltpu

EPS = 1e-6
CONV_WIDTH = 31
POOL_WINDOWS = (2, 4, 8, 16)
XATTN_HEADS = 4
FFN_CONV_WIDTH = 3

SUBLANES = 8
LANES = 128
MXU_COLS = 256
VMEM_LIMIT_BYTES = 56 * 1024 * 1024

SEQ_TILE = 512
CONV_HALO = 32
POOL_HALO = 16
FFN_CHUNK = MXU_COLS

BF16 = jnp.bfloat16
F32 = jnp.float32


def _rmsnorm(x, g):
    ms = jnp.mean(x * x, axis=-1, keepdims=True)
    return x * lax.rsqrt(ms + EPS) * g


def _dot(a, b):
    return jnp.dot(a, b, preferred_element_type=F32)


def _resident(shape):
    zeros = (0,) * len(shape)
    return pl.BlockSpec(shape, lambda b, s: zeros, pipeline_mode=pl.Buffered(1))


def _params(n_axes):
    return pltpu.CompilerParams(
        dimension_semantics=("arbitrary",) * n_axes,
        vmem_limit_bytes=VMEM_LIMIT_BYTES)


def _kv_body(mem_ref, g_ref, wkv_ref, k_ref, v_ref, *, d_model, scale):
    mem_n = _rmsnorm(mem_ref[0], g_ref[...]).astype(BF16)
    kv = _dot(mem_n, wkv_ref[...])
    k_ref[0] = (kv[:, :d_model] * scale).astype(BF16)
    v_ref[0] = kv[:, d_model:].astype(BF16)


def _kv_call(mem, g, wkv, scale):
    b, n_mem, d = mem.shape
    return pl.pallas_call(
        functools.partial(_kv_body, d_model=d, scale=scale),
        grid=(b, 1),
        in_specs=[
            pl.BlockSpec((1, n_mem, d), lambda i, s: (i, 0, 0)),
            _resident((1, d)),
            _resident((d, 2 * d)),
        ],
        out_specs=[
            pl.BlockSpec((1, n_mem, d), lambda i, s: (i, 0, 0)),
            pl.BlockSpec((1, n_mem, d), lambda i, s: (i, 0, 0)),
        ],
        out_shape=[jax.ShapeDtypeStruct((b, n_mem, d), BF16)] * 2,
        compiler_params=_params(2),
        name="kv_proj",
    )(mem, g, wkv)


def _mixer_body(x_ref, g_ref, win_ref, dww_ref, dwb_ref, lng_ref, lnb_ref,
                poolw_ref, pools_ref, wout_ref, o_ref, conv_ext, pool_ext,
                *, ts, d_conv, d_pool):
    s = pl.program_id(1)

    @pl.when(s == 0)
    def _():
        conv_ext[0:CONV_HALO, :] = jnp.zeros((CONV_HALO, d_conv), F32)
        pool_ext[0:POOL_HALO, :] = jnp.zeros((POOL_HALO, d_pool), F32)

    x = x_ref[0]
    h = _rmsnorm(x, g_ref[...]).astype(BF16)
    u = _dot(h, win_ref[...])

    val = u[:, :d_conv]
    gate = u[:, d_conv:2 * d_conv]
    conv_ext[CONV_HALO:CONV_HALO + ts, :] = val * jax.nn.sigmoid(gate)
    acc = jnp.zeros((ts, d_conv), F32) + dwb_ref[...]
    for k in range(CONV_WIDTH):
        start = CONV_HALO - (CONV_WIDTH - 1) + k
        acc = acc + dww_ref[k:k + 1, :] * conv_ext[start:start + ts, :]
    conv_ext[0:CONV_HALO, :] = conv_ext[ts:ts + CONV_HALO, :]
    mu = jnp.mean(acc, axis=-1, keepdims=True)
    cen = acc - mu
    var = jnp.mean(cen * cen, axis=-1, keepdims=True)
    yc = cen * lax.rsqrt(var + EPS) * lng_ref[...] + lnb_ref[...]
    yc = yc * jax.nn.sigmoid(yc)

    pool_ext[POOL_HALO:POOL_HALO + ts, :] = u[:, 2 * d_conv:]
    pos = (s * ts + lax.broadcasted_iota(jnp.int32, (ts, 1), 0) + 1).astype(F32)
    gd = d_pool // len(POOL_WINDOWS)
    mixed = []
    for gi, w in enumerate(POOL_WINDOWS):
        cols = slice(gi * gd, (gi + 1) * gd)
        v = pool_ext[POOL_HALO:POOL_HALO + ts, cols]
        tot = v
        for i in range(1, w):
            tot = tot + pool_ext[POOL_HALO - i:POOL_HALO - i + ts, cols]
        pooled = tot / jnp.minimum(pos, float(w)) - v
        mixed.append(_dot(pooled.astype(BF16), poolw_ref[gi]))
    pool_ext[0:POOL_HALO, :] = pool_ext[ts:ts + POOL_HALO, :]
    yp = jnp.concatenate(mixed, axis=-1) * pools_ref[...]

    y = jnp.concatenate([yc, yp], axis=-1).astype(BF16)
    o_ref[0] = x + _dot(y, wout_ref[...])


def _mixer_call(x, g, win, dww, dwb, lng, lnb, poolw, pools, wout):
    b, seq, d = x.shape
    d_conv = dww.shape[1]
    d_pool = pools.shape[1]
    ts = SEQ_TILE
    tile = pl.BlockSpec((1, ts, d), lambda i, s: (i, s, 0))
    return pl.pallas_call(
        functools.partial(_mixer_body, ts=ts, d_conv=d_conv, d_pool=d_pool),
        grid=(b, seq // ts),
        in_specs=[tile, _resident(g.shape), _resident(win.shape),
                  _resident(dww.shape), _resident(dwb.shape),
                  _resident(lng.shape), _resident(lnb.shape),
                  _resident(poolw.shape), _resident(pools.shape),
                  _resident(wout.shape)],
        out_specs=tile,
        out_shape=jax.ShapeDtypeStruct(x.shape, x.dtype),
        scratch_shapes=[pltpu.VMEM((CONV_HALO + ts, d_conv), F32),
                        pltpu.VMEM((POOL_HALO + ts, d_pool), F32)],
        compiler_params=_params(2),
        name="mixer",
    )(x, g, win, dww, dwb, lng, lnb, poolw, pools, wout)


def _attn_body(x_ref, g_ref, wq_ref, k_ref, v_ref, wo_ref, o_ref, *, heads):
    x = x_ref[0]
    d = x.shape[-1]
    hd = d // heads
    h = _rmsnorm(x, g_ref[...]).astype(BF16)
    q = _dot(h, wq_ref[...]).astype(BF16)
    outs = []
    for i in range(heads):
        cols = slice(i * hd, (i + 1) * hd)
        sc = lax.dot_general(q[:, cols], k_ref[0, :, cols],
                             (((1,), (1,)), ((), ())),
                             preferred_element_type=F32)
        p = jnp.exp(sc - jnp.max(sc, axis=-1, keepdims=True))
        inv = 1.0 / jnp.sum(p, axis=-1, keepdims=True)
        outs.append(_dot(p.astype(BF16), v_ref[0, :, cols]) * inv)
    o = jnp.concatenate(outs, axis=-1).astype(BF16)
    o_ref[0] = x + _dot(o, wo_ref[...])


def _attn_call(x, g, wq, k, v, wo, heads):
    b, seq, d = x.shape
    n_mem = k.shape[1]
    ts = SEQ_TILE
    tile = pl.BlockSpec((1, ts, d), lambda i, s: (i, s, 0))
    mem_tile = pl.BlockSpec((1, n_mem, d), lambda i, s: (i, 0, 0))
    return pl.pallas_call(
        functools.partial(_attn_body, heads=heads),
        grid=(b, seq // ts),
        in_specs=[tile, _resident(g.shape), _resident(wq.shape),
                  mem_tile, mem_tile, _resident(wo.shape)],
        out_specs=tile,
        out_shape=jax.ShapeDtypeStruct(x.shape, x.dtype),
        compiler_params=_params(2),
        name="xattn",
    )(x, g, wq, k, v, wo)


def _ffn_body(x_ref, g_ref, wup_ref, dww_ref, dwb_ref, wdown_ref, gf_ref,
              o_ref, carry, act, *, ts, d_ff, final_norm):
    s = pl.program_id(1)

    @pl.when(s == 0)
    def _():
        carry[...] = jnp.zeros(carry.shape, F32)

    x = x_ref[0]
    h = _rmsnorm(x, g_ref[...]).astype(BF16)
    row = lax.broadcasted_iota(jnp.int32, (SUBLANES, 1), 0)

    def conv(c0, width):
        cols = slice(c0, c0 + width)
        up = _dot(h, wup_ref[:, cols])
        prev1 = carry[SUBLANES - 1:SUBLANES, cols]
        prev2 = carry[SUBLANES - 2:SUBLANES - 1, cols]
        carry[:, cols] = up[ts - SUBLANES:, :]
        r1 = pltpu.roll(up, 1, axis=0)
        r2 = pltpu.roll(up, 2, axis=0)
        r1 = jnp.concatenate(
            [jnp.where(row == 0, prev1, r1[:SUBLANES]), r1[SUBLANES:]], axis=0)
        r2 = jnp.concatenate(
            [jnp.where(row == 0, prev2, jnp.where(row == 1, prev1, r2[:SUBLANES])),
             r2[SUBLANES:]], axis=0)
        return (dww_ref[0:1, cols] * r2 + dww_ref[1:2, cols] * r1
                + dww_ref[2:3, cols] * up + dwb_ref[:, cols])

    for c0 in range(0, d_ff, FFN_CHUNK):
        width = min(FFN_CHUNK, d_ff - c0)
        gate = conv(c0, width)
        val = conv(d_ff + c0, width)
        act[:, c0:c0 + width] = (gate * jax.nn.sigmoid(gate) * val).astype(BF16)

    y = x + _dot(act[...], wdown_ref[...])
    if final_norm:
        y = _rmsnorm(y, gf_ref[...])
    o_ref[0] = y


def _ffn_call(x, g, wup, dww, dwb, wdown, gf, final_norm):
    b, seq, d = x.shape
    d_ff = wdown.shape[0]
    ts = SEQ_TILE
    tile = pl.BlockSpec((1, ts, d), lambda i, s: (i, s, 0))
    return pl.pallas_call(
        functools.partial(_ffn_body, ts=ts, d_ff=d_ff, final_norm=final_norm),
        grid=(b, seq // ts),
        in_specs=[tile, _resident(g.shape), _resident(wup.shape),
                  _resident(dww.shape), _resident(dwb.shape),
                  _resident(wdown.shape), _resident(gf.shape)],
        out_specs=tile,
        out_shape=jax.ShapeDtypeStruct(x.shape, x.dtype),
        scratch_shapes=[pltpu.VMEM((SUBLANES, 2 * d_ff), F32),
                        pltpu.VMEM((ts, d_ff), BF16)],
        compiler_params=_params(2),
        name="convffn",
    )(x, g, wup, dww, dwb, wdown, gf)


def kernel(x, mem, norm_mix_g, w_in, conv_dw_w, conv_dw_b, conv_ln_g, conv_ln_b,
           pool_w, pool_scale, w_out, norm_xattn_g, norm_mem_g, w_q, w_kv, w_o,
           norm_ffn_g, w_up, ffn_dw_w, ffn_dw_b, w_down, norm_final_g):
    depth = w_in.shape[0]
    d = x.shape[-1]
    assert x.shape[1] % SEQ_TILE == 0
    assert conv_dw_w.shape[1] == CONV_WIDTH and ffn_dw_w.shape[1] == FFN_CONV_WIDTH
    scale = float(d // XATTN_HEADS) ** -0.5

    def row(v):
        return v.reshape(1, -1)

    gf = row(norm_final_g)
    for l in range(depth):
        x = _mixer_call(
            x, row(norm_mix_g[l]), w_in[l].astype(BF16), conv_dw_w[l],
            row(conv_dw_b[l]), row(conv_ln_g[l]), row(conv_ln_b[l]),
            pool_w[l].astype(BF16), row(pool_scale[l]), w_out[l].astype(BF16))
        k, v = _kv_call(mem, row(norm_mem_g[l]), w_kv[l].astype(BF16), scale)
        x = _attn_call(x, row(norm_xattn_g[l]), w_q[l].astype(BF16), k, v,
                       w_o[l].astype(BF16), XATTN_HEADS)
        x = _ffn_call(x, row(norm_ffn_g[l]), w_up[l].astype(BF16), ffn_dw_w[l],
                      row(ffn_dw_b[l]), w_down[l].astype(BF16), gf,
                      final_norm=(l == depth - 1))
    return x
```

```python
import functools
import math

import jax
import jax.numpy as jnp
from jax import lax
from jax.experimental import pallas as pl
from jax.experimental.pallas import tpu as pltpu

EPS = 1e-6
CONV_WIDTH = 31
POOL_WINDOWS = (2, 4, 8, 16)
XATTN_HEADS = 4
FFN_CONV_WIDTH = 3

SUBLANES = 8
LANES = 128
MXU_COLS = 256
VMEM_LIMIT_BYTES = 56 * 1024 * 1024

SEQ_TILE = 512
CONV_HALO = 32
CONV_ROWS = 64
POOL_HALO = 16
FFN_CHUNK = MXU_COLS

BF16 = jnp.bfloat16
F32 = jnp.float32


def _rmsnorm(x, g):
    ms = jnp.mean(x * x, axis=-1, keepdims=True)
    return x * lax.rsqrt(ms + EPS) * g


def _dot(a, b):
    return jnp.dot(a, b, preferred_element_type=F32)


def _shift_down(a, b, head=None):
    n, c = a.shape
    a3 = a.reshape(n // SUBLANES, SUBLANES, c)
    rot = pltpu.roll(a3, b, axis=1)
    first = rot[-1:] if head is None else pltpu.roll(head, b, axis=0)[None]
    prev = jnp.concatenate([first, rot[:-1]], axis=0)
    sub = lax.broadcasted_iota(jnp.int32, (1, SUBLANES, 1), 1)
    return jnp.where(sub >= b, rot, prev).reshape(n, c)


def _window_sum(e, w):
    assert w & (w - 1) == 0
    tot, step = e, 1
    while step < w:
        if step % SUBLANES == 0:
            shifted = jnp.concatenate([tot[-step:], tot[:-step]], axis=0)
        else:
            shifted = _shift_down(tot, step)
        tot = tot + shifted
        step *= 2
    return tot


def _resident(shape):
    zeros = (0,) * len(shape)
    return pl.BlockSpec(shape, lambda b, s: zeros, pipeline_mode=pl.Buffered(1))


def _params(n_axes):
    return pltpu.CompilerParams(
        dimension_semantics=("arbitrary",) * n_axes,
        vmem_limit_bytes=VMEM_LIMIT_BYTES)


def _kv_body(mem_ref, g_ref, wkv_ref, k_ref, v_ref, *, d_model, scale):
    mem_n = _rmsnorm(mem_ref[0], g_ref[...]).astype(BF16)
    kv = _dot(mem_n, wkv_ref[...])
    k_ref[0] = (kv[:, :d_model] * scale).astype(BF16)
    v_ref[0] = kv[:, d_model:].astype(BF16)


def _kv_call(mem, g, wkv, scale):
    b, n_mem, d = mem.shape
    return pl.pallas_call(
        functools.partial(_kv_body, d_model=d, scale=scale),
        grid=(b, 1),
        in_specs=[
            pl.BlockSpec((1, n_mem, d), lambda i, s: (i, 0, 0)),
            _resident((1, d)),
            _resident((d, 2 * d)),
        ],
        out_specs=[
            pl.BlockSpec((1, n_mem, d), lambda i, s: (i, 0, 0)),
            pl.BlockSpec((1, n_mem, d), lambda i, s: (i, 0, 0)),
        ],
        out_shape=[jax.ShapeDtypeStruct((b, n_mem, d), BF16)] * 2,
        compiler_params=_params(2),
        name="kv_proj",
    )(mem, g, wkv)


def _mixer_body(x_ref, g_ref, win_ref, dww_ref, dwb_ref, lng_ref, lnb_ref,
                poolw_ref, pools_ref, wout_ref, o_ref, conv_ext, conv_sh,
                pool_ext, y_scr, *, ts, d_conv, d_pool):
    s = pl.program_id(1)

    @pl.when(s == 0)
    def _():
        conv_ext[0:CONV_HALO, :] = jnp.zeros((CONV_HALO, d_conv), F32)
        pool_ext[0:POOL_HALO, :] = jnp.zeros((POOL_HALO, d_pool), F32)

    x = x_ref[0]
    h = _rmsnorm(x, g_ref[...]).astype(BF16)
    u = _dot(h, win_ref[...])

    val = u[:, :d_conv]
    gate = u[:, d_conv:2 * d_conv]
    conv_ext[CONV_HALO:CONV_HALO + ts, :] = val * jax.nn.sigmoid(gate)
    ext = conv_ext[...]
    for b in range(1, SUBLANES):
        conv_sh[b - 1] = _shift_down(ext, b)
    for r0 in range(0, ts, CONV_ROWS):
        acc = jnp.broadcast_to(dwb_ref[...], (CONV_ROWS, d_conv))
        for delay in range(CONV_WIDTH):
            a, b = divmod(delay, SUBLANES)
            start = CONV_HALO + r0 - SUBLANES * a
            src = conv_ext if b == 0 else conv_sh.at[b - 1]
            k = CONV_WIDTH - 1 - delay
            acc = acc + dww_ref[k:k + 1, :] * src[start:start + CONV_ROWS, :]
        mu = jnp.mean(acc, axis=-1, keepdims=True)
        cen = acc - mu
        var = jnp.mean(cen * cen, axis=-1, keepdims=True)
        yc = cen * lax.rsqrt(var + EPS) * lng_ref[...] + lnb_ref[...]
        y_scr[r0:r0 + CONV_ROWS, 0:d_conv] = (yc * jax.nn.sigmoid(yc)).astype(BF16)
    conv_ext[0:CONV_HALO, :] = conv_ext[ts:ts + CONV_HALO, :]

    pool_ext[POOL_HALO:POOL_HALO + ts, :] = u[:, 2 * d_conv:]
    pos = (s * ts + lax.broadcasted_iota(jnp.int32, (ts, 1), 0) + 1).astype(F32)
    gd = d_pool // len(POOL_WINDOWS)
    for gi, w in enumerate(POOL_WINDOWS):
        cols = slice(gi * gd, (gi + 1) * gd)
        e = pool_ext[:, cols]
        v = e[POOL_HALO:]
        inv_count = 1.0 / jnp.minimum(pos, float(w))
        pooled = _window_sum(e, w)[POOL_HALO:] * inv_count - v
        mixed = _dot(pooled.astype(BF16), poolw_ref[gi]) * pools_ref[:, cols]
        y_scr[:, d_conv + gi * gd:d_conv + (gi + 1) * gd] = mixed.astype(BF16)
    pool_ext[0:POOL_HALO, :] = pool_ext[ts:ts + POOL_HALO, :]

    o_ref[0] = x + _dot(y_scr[...], wout_ref[...])


def _mixer_call(x, g, win, dww, dwb, lng, lnb, poolw, pools, wout):
    b, seq, d = x.shape
    d_conv = dww.shape[1]
    d_pool = pools.shape[1]
    ts = SEQ_TILE
    tile = pl.BlockSpec((1, ts, d), lambda i, s: (i, s, 0))
    return pl.pallas_call(
        functools.partial(_mixer_body, ts=ts, d_conv=d_conv, d_pool=d_pool),
        grid=(b, seq // ts),
        in_specs=[tile, _resident(g.shape), _resident(win.shape),
                  _resident(dww.shape), _resident(dwb.shape),
                  _resident(lng.shape), _resident(lnb.shape),
                  _resident(poolw.shape), _resident(pools.shape),
                  _resident(wout.shape)],
        out_specs=tile,
        out_shape=jax.ShapeDtypeStruct(x.shape, x.dtype),
        scratch_shapes=[pltpu.VMEM((CONV_HALO + ts, d_conv), F32),
                        pltpu.VMEM((SUBLANES - 1, CONV_HALO + ts, d_conv), F32),
                        pltpu.VMEM((POOL_HALO + ts, d_pool), F32),
                        pltpu.VMEM((ts, d_conv + d_pool), BF16)],
        compiler_params=_params(2),
        name="mixer",
    )(x, g, win, dww, dwb, lng, lnb, poolw, pools, wout)


def _attn_body(x_ref, g_ref, wq_ref, k_ref, v_ref, wo_ref, o_ref, *, heads):
    x = x_ref[0]
    d = x.shape[-1]
    hd = d // heads
    h = _rmsnorm(x, g_ref[...]).astype(BF16)
    q = _dot(h, wq_ref[...]).astype(BF16)
    outs = []
    for i in range(heads):
        cols = slice(i * hd, (i + 1) * hd)
        sc = lax.dot_general(q[:, cols], k_ref[0, :, cols],
                             (((1,), (1,)), ((), ())),
                             preferred_element_type=F32)
        p = jnp.exp(sc - jnp.max(sc, axis=-1, keepdims=True))
        inv = 1.0 / jnp.sum(p, axis=-1, keepdims=True)
        outs.append(_dot(p.astype(BF16), v_ref[0, :, cols]) * inv)
    o = jnp.concatenate(outs, axis=-1).astype(BF16)
    o_ref[0] = x + _dot(o, wo_ref[...])


def _attn_call(x, g, wq, k, v, wo, heads):
    b, seq, d = x.shape
    n_mem = k.shape[1]
    ts = SEQ_TILE
    tile = pl.BlockSpec((1, ts, d), lambda i, s: (i, s, 0))
    mem_tile = pl.BlockSpec((1, n_mem, d), lambda i, s: (i, 0, 0))
    return pl.pallas_call(
        functools.partial(_attn_body, heads=heads),
        grid=(b, seq // ts),
        in_specs=[tile, _resident(g.shape), _resident(wq.shape),
                  mem_tile, mem_tile, _resident(wo.shape)],
        out_specs=tile,
        out_shape=jax.ShapeDtypeStruct(x.shape, x.dtype),
        compiler_params=_params(2),
        name="xattn",
    )(x, g, wq, k, v, wo)


def _ffn_body(x_ref, g_ref, wup_ref, dww_ref, dwb_ref, wdown_ref, gf_ref,
              o_ref, carry, act, *, ts, d_ff, final_norm):
    s = pl.program_id(1)

    @pl.when(s == 0)
    def _():
        carry[...] = jnp.zeros(carry.shape, F32)

    x = x_ref[0]
    h = _rmsnorm(x, g_ref[...]).astype(BF16)

    def conv(c0, width):
        cols = slice(c0, c0 + width)
        up = _dot(h, wup_ref[:, cols])
        head = carry[:, cols]
        carry[:, cols] = up[ts - SUBLANES:, :]
        return (dww_ref[0:1, cols] * _shift_down(up, 2, head)
                + dww_ref[1:2, cols] * _shift_down(up, 1, head)
                + dww_ref[2:3, cols] * up + dwb_ref[:, cols])

    for c0 in range(0, d_ff, FFN_CHUNK):
        width = min(FFN_CHUNK, d_ff - c0)
        gate = conv(c0, width)
        val = conv(d_ff + c0, width)
        act[:, c0:c0 + width] = (gate * jax.nn.sigmoid(gate) * val).astype(BF16)

    y = x + _dot(act[...], wdown_ref[...])
    if final_norm:
        y = _rmsnorm(y, gf_ref[...])
    o_ref[0] = y


def _ffn_call(x, g, wup, dww, dwb, wdown, gf, final_norm):
    b, seq, d = x.shape
    d_ff = wdown.shape[0]
    ts = SEQ_TILE
    tile = pl.BlockSpec((1, ts, d), lambda i, s: (i, s, 0))
    return pl.pallas_call(
        functools.partial(_ffn_body, ts=ts, d_ff=d_ff, final_norm=final_norm),
        grid=(b, seq // ts),
        in_specs=[tile, _resident(g.shape), _resident(wup.shape),
                  _resident(dww.shape), _resident(dwb.shape),
                  _resident(wdown.shape), _resident(gf.shape)],
        out_specs=tile,
        out_shape=jax.ShapeDtypeStruct(x.shape, x.dtype),
        scratch_shapes=[pltpu.VMEM((SUBLANES, 2 * d_ff), F32),
                        pltpu.VMEM((ts, d_ff), BF16)],
        compiler_params=_params(2),
        name="convffn",
    )(x, g, wup, dww, dwb, wdown, gf)


def kernel(x, mem, norm_mix_g, w_in, conv_dw_w, conv_dw_b, conv_ln_g, conv_ln_b,
           pool_w, pool_scale, w_out, norm_xattn_g, norm_mem_g, w_q, w_kv, w_o,
           norm_ffn_g, w_up, ffn_dw_w, ffn_dw_b, w_down, norm_final_g):
    depth = w_in.shape[0]
    d = x.shape[-1]
    assert x.shape[1] % SEQ_TILE == 0 and SEQ_TILE % CONV_ROWS == 0
    assert conv_dw_w.shape[1] == CONV_WIDTH and ffn_dw_w.shape[1] == FFN_CONV_WIDTH
    assert CONV_WIDTH - 1 <= CONV_HALO and max(POOL_WINDOWS) - 1 <= POOL_HALO
    scale = float(d // XATTN_HEADS) ** -0.5
    assert math.frexp(scale)[0] == 0.5

    def row(v):
        return v.reshape(1, -1)

    gf = row(norm_final_g)
    for l in range(depth):
        x = _mixer_call(
            x, row(norm_mix_g[l]), w_in[l].astype(BF16), conv_dw_w[l],
            row(conv_dw_b[l]), row(conv_ln_g[l]), row(conv_ln_b[l]),
            pool_w[l].astype(BF16), row(pool_scale[l]), w_out[l].astype(BF16))
        k, v = _kv_call(mem, row(norm_mem_g[l]), w_kv[l].astype(BF16), scale)
        x = _attn_call(x, row(norm_xattn_g[l]), w_q[l].astype(BF16), k, v,
                       w_o[l].astype(BF16), XATTN_HEADS)
        x = _ffn_call(x, row(norm_ffn_g[l]), w_up[l].astype(BF16), ffn_dw_w[l],
                      row(ffn_dw_b[l]), w_down[l].astype(BF16), gf,
                      final_norm=(l == depth - 1))
    return x
```

```python
import functools
import math

import jax
import jax.numpy as jnp
from jax import lax
from jax.experimental import pallas as pl
from jax.experimental.pallas import tpu as pltpu

EPS = 1e-6
CONV_WIDTH = 31
POOL_WINDOWS = (2, 4, 8, 16)
XATTN_HEADS = 4
FFN_CONV_WIDTH = 3

SUBLANES = 8
LANES = 128
MXU_COLS = 256
VMEM_LIMIT_BYTES = 60 * 1024 * 1024

SEQ_TILE = 512
CONV_HALO = 32
CONV_ROWS = 64
POOL_HALO = 16
FFN_CHUNK = 2 * MXU_COLS
PIPE_DEPTH = 2

BF16 = jnp.bfloat16
F32 = jnp.float32


def _rmsnorm(x, g):
    ms = jnp.mean(x * x, axis=-1, keepdims=True)
    return x * lax.rsqrt(ms + EPS) * g


def _dot(a, b):
    return jnp.dot(a, b, preferred_element_type=F32)


def _shift_down(a, b, head=None):
    n, c = a.shape
    a3 = a.reshape(n // SUBLANES, SUBLANES, c)
    rot = pltpu.roll(a3, b, axis=1)
    first = rot[-1:] if head is None else pltpu.roll(head, b, axis=0)[None]
    prev = jnp.concatenate([first, rot[:-1]], axis=0)
    sub = lax.broadcasted_iota(jnp.int32, (1, SUBLANES, 1), 1)
    return jnp.where(sub >= b, rot, prev).reshape(n, c)


def _window_sum(e, w):
    assert w & (w - 1) == 0
    tot, step = e, 1
    while step < w:
        if step % SUBLANES == 0:
            shifted = jnp.concatenate([tot[-step:], tot[:-step]], axis=0)
        else:
            shifted = _shift_down(tot, step)
        tot = tot + shifted
        step *= 2
    return tot


def _resident(shape):
    zeros = (0,) * len(shape)
    return pl.BlockSpec(shape, lambda *_: zeros, pipeline_mode=pl.Buffered(1))


def _params(n_axes):
    return pltpu.CompilerParams(
        dimension_semantics=("arbitrary",) * n_axes,
        vmem_limit_bytes=VMEM_LIMIT_BYTES)


def _kv_body(mem_ref, g_ref, wkv_ref, k_ref, v_ref, *, d_model, scale):
    mem_n = _rmsnorm(mem_ref[0], g_ref[...]).astype(BF16)
    kv = _dot(mem_n, wkv_ref[...])
    k_ref[0] = (kv[:, :d_model] * scale).astype(BF16)
    v_ref[0] = kv[:, d_model:].astype(BF16)


def _kv_call(mem, g, wkv, scale):
    b, n_mem, d = mem.shape
    return pl.pallas_call(
        functools.partial(_kv_body, d_model=d, scale=scale),
        grid=(b,),
        in_specs=[
            pl.BlockSpec((1, n_mem, d), lambda i: (i, 0, 0)),
            _resident((1, d)),
            _resident((d, 2 * d)),
        ],
        out_specs=[
            pl.BlockSpec((1, n_mem, d), lambda i: (i, 0, 0)),
            pl.BlockSpec((1, n_mem, d), lambda i: (i, 0, 0)),
        ],
        out_shape=[jax.ShapeDtypeStruct((b, n_mem, d), BF16)] * 2,
        compiler_params=_params(1),
        name="kv_proj",
    )(mem, g, wkv)


def _mixer_stage(x_ref, x1buf, seq_tile, g_ref, win_ref, dww_ref, dwb_ref, lng_ref,
                 lnb_ref, poolw_ref, pools_ref, wout_ref, conv_ext, conv_sh, pool_ext,
                 y_scr):
    ts = x1buf.shape[0]
    d_conv = conv_ext.shape[1]
    d_pool = pool_ext.shape[1]
    h = _rmsnorm(x_ref[0], g_ref[...]).astype(BF16)
    u = _dot(h, win_ref[...])
    yield

    val = u[:, :d_conv]
    gate = u[:, d_conv:2 * d_conv]
    conv_ext[CONV_HALO:CONV_HALO + ts, :] = val * jax.nn.sigmoid(gate)
    pool_ext[POOL_HALO:POOL_HALO + ts, :] = u[:, 2 * d_conv:]
    ext = conv_ext[...]
    for b in range(1, SUBLANES):
        conv_sh[b - 1] = _shift_down(ext, b)
    yield

    for r0 in range(0, ts, CONV_ROWS):
        acc = jnp.broadcast_to(dwb_ref[...], (CONV_ROWS, d_conv))
        acc = acc.reshape(CONV_ROWS // SUBLANES, SUBLANES, d_conv)
        for delay in range(CONV_WIDTH):
            a, b = divmod(delay, SUBLANES)
            start = CONV_HALO + r0 - SUBLANES * a
            src = conv_ext if b == 0 else conv_sh.at[b - 1]
            tap = src[start:start + CONV_ROWS, :]
            acc = acc + dww_ref[CONV_WIDTH - 1 - delay][None] * tap.reshape(acc.shape)
        acc = acc.reshape(CONV_ROWS, d_conv)
        mu = jnp.mean(acc, axis=-1, keepdims=True)
        cen = acc - mu
        var = jnp.mean(cen * cen, axis=-1, keepdims=True)
        yc = cen * lax.rsqrt(var + EPS) * lng_ref[...] + lnb_ref[...]
        y_scr[r0:r0 + CONV_ROWS, 0:d_conv] = (yc * jax.nn.sigmoid(yc)).astype(BF16)
        yield
    conv_ext[0:CONV_HALO, :] = conv_ext[ts:ts + CONV_HALO, :]

    pos = (seq_tile * ts + lax.broadcasted_iota(jnp.int32, (ts, 1), 0) + 1).astype(F32)
    gd = d_pool // len(POOL_WINDOWS)
    for gi, w in enumerate(POOL_WINDOWS):
        cols = slice(gi * gd, (gi + 1) * gd)
        e = pool_ext[:, cols]
        v = e[POOL_HALO:]
        inv_count = 1.0 / jnp.minimum(pos, float(w))
        pooled = _window_sum(e, w)[POOL_HALO:] * inv_count - v
        mixed = _dot(pooled.astype(BF16), poolw_ref[gi]) * pools_ref[:, cols]
        y_scr[:, d_conv + gi * gd:d_conv + (gi + 1) * gd] = mixed.astype(BF16)
    pool_ext[0:POOL_HALO, :] = pool_ext[ts:ts + POOL_HALO, :]
    yield

    x1buf[...] = x_ref[0] + _dot(y_scr[...], wout_ref[...])
    yield


def _attn_stage(x1buf, x2buf, g_ref, wq_ref, k_ref, v_ref, wo_ref, o_scr):
    d = x1buf.shape[-1]
    hd = d // XATTN_HEADS
    h = _rmsnorm(x1buf[...], g_ref[...]).astype(BF16)
    q = _dot(h, wq_ref[...]).astype(BF16)
    yield
    for i in range(XATTN_HEADS):
        cols = slice(i * hd, (i + 1) * hd)
        sc = lax.dot_general(q[:, cols], k_ref[0, :, cols],
                             (((1,), (1,)), ((), ())),
                             preferred_element_type=F32)
        p = jnp.exp(sc - jnp.max(sc, axis=-1, keepdims=True))
        inv = 1.0 / jnp.sum(p, axis=-1, keepdims=True)
        o_scr[:, cols] = (_dot(p.astype(BF16), v_ref[0, :, cols]) * inv).astype(BF16)
        yield
    x2buf[...] = x1buf[...] + _dot(o_scr[...], wo_ref[...])
    yield


def _ffn_stage(x2buf, o_ref, g_ref, wup_ref, dww_ref, dwb_ref, wdown_ref, gf_ref, carry,
               act, final_norm):
    ts = x2buf.shape[0]
    d_ff = act.shape[1]
    h = _rmsnorm(x2buf[...], g_ref[...]).astype(BF16)
    yield

    def conv(c0, width):
        cols = slice(c0, c0 + width)
        up = _dot(h, wup_ref[:, cols])
        head = carry[:, cols]
        carry[:, cols] = up[ts - SUBLANES:, :]
        return (dww_ref[0:1, cols] * _shift_down(up, 2, head)
                + dww_ref[1:2, cols] * _shift_down(up, 1, head)
                + dww_ref[2:3, cols] * up + dwb_ref[:, cols])

    for c0 in range(0, d_ff, FFN_CHUNK):
        width = min(FFN_CHUNK, d_ff - c0)
        gate = conv(c0, width)
        val = conv(d_ff + c0, width)
        act[:, c0:c0 + width] = (gate * jax.nn.sigmoid(gate) * val).astype(BF16)
        yield

    y = x2buf[...] + _dot(act[...], wdown_ref[...])
    if final_norm:
        y = _rmsnorm(y, gf_ref[...])
    o_ref[0] = y
    yield


STAGE_ORDER = "MFAM" "FAFAMFAFAM" "FFMF" "MMMMM" "A" "M" "M"


def _layer_body(x_ref, k_ref, v_ref,
                gm_ref, win_ref, cdw_ref, cdb_ref, lng_ref, lnb_ref, poolw_ref,
                pools_ref, wout_ref,
                ga_ref, wq_ref, wo_ref,
                gff_ref, wup_ref, fdw_ref, fdb_ref, wdown_ref, gfin_ref,
                o_ref,
                conv_ext, conv_sh, pool_ext, y_scr, o_scr, x1buf, x2buf, carry, act,
                *, tiles_per_row, final_norm):
    i = pl.program_id(0)
    seq_tile = i % tiles_per_row

    @pl.when(i == 0)
    def _():
        x1buf[...] = jnp.zeros(x1buf.shape, F32)
        x2buf[...] = jnp.zeros(x2buf.shape, F32)

    @pl.when(seq_tile == 0)
    def _():
        conv_ext[0:CONV_HALO, :] = jnp.zeros((CONV_HALO, conv_ext.shape[1]), F32)
        pool_ext[0:POOL_HALO, :] = jnp.zeros((POOL_HALO, pool_ext.shape[1]), F32)

    @pl.when(jnp.logical_or(i == 0, seq_tile == PIPE_DEPTH % tiles_per_row))
    def _():
        carry[...] = jnp.zeros(carry.shape, F32)

    stages = {
        "F": _ffn_stage(x2buf, o_ref, gff_ref, wup_ref, fdw_ref, fdb_ref, wdown_ref,
                        gfin_ref, carry, act, final_norm),
        "A": _attn_stage(x1buf, x2buf, ga_ref, wq_ref, k_ref, v_ref, wo_ref, o_scr),
        "M": _mixer_stage(x_ref, x1buf, seq_tile, gm_ref, win_ref, cdw_ref, cdb_ref,
                          lng_ref, lnb_ref, poolw_ref, pools_ref, wout_ref, conv_ext,
                          conv_sh, pool_ext, y_scr),
    }
    for name in STAGE_ORDER:
        next(stages[name])
    done = object()
    for gen in stages.values():
        assert next(gen, done) is done, "STAGE_ORDER does not cover every piece"


def _layer_call(x, k, v, mixer_w, attn_w, ffn_w, final_norm):
    b, seq, d = x.shape
    n_mem = k.shape[1]
    ts = SEQ_TILE
    tiles_per_row = seq // ts
    n_tiles = b * tiles_per_row
    d_conv = mixer_w[2].shape[-1]
    d_pool = mixer_w[7].shape[-1]
    d_ff = ffn_w[4].shape[0]

    def tile_of(step):
        t = jnp.clip(step, 0, n_tiles - 1)
        return t // tiles_per_row, t % tiles_per_row

    def x_map(i):
        row, col = tile_of(i)
        return row, col, 0

    def kv_map(i):
        row, _ = tile_of(i - 1)
        return row, 0, 0

    def o_map(i):
        row, col = tile_of(i - PIPE_DEPTH)
        return row, col, 0

    weights = (*mixer_w, *attn_w, *ffn_w)
    return pl.pallas_call(
        functools.partial(_layer_body, tiles_per_row=tiles_per_row,
                          final_norm=final_norm),
        grid=(n_tiles + PIPE_DEPTH,),
        in_specs=[pl.BlockSpec((1, ts, d), x_map),
                  pl.BlockSpec((1, n_mem, d), kv_map),
                  pl.BlockSpec((1, n_mem, d), kv_map),
                  *[_resident(w.shape) for w in weights]],
        out_specs=pl.BlockSpec((1, ts, d), o_map),
        out_shape=jax.ShapeDtypeStruct(x.shape, x.dtype),
        scratch_shapes=[pltpu.VMEM((CONV_HALO + ts, d_conv), F32),
                        pltpu.VMEM((SUBLANES - 1, CONV_HALO + ts, d_conv), F32),
                        pltpu.VMEM((POOL_HALO + ts, d_pool), F32),
                        pltpu.VMEM((ts, d_conv + d_pool), BF16),
                        pltpu.VMEM((ts, d), BF16),
                        pltpu.VMEM((ts, d), F32),
                        pltpu.VMEM((ts, d), F32),
                        pltpu.VMEM((SUBLANES, 2 * d_ff), F32),
                        pltpu.VMEM((ts, d_ff), BF16)],
        compiler_params=_params(1),
        name="layer",
    )(x, k, v, *weights)


def kernel(x, mem, norm_mix_g, w_in, conv_dw_w, conv_dw_b, conv_ln_g, conv_ln_b,
           pool_w, pool_scale, w_out, norm_xattn_g, norm_mem_g, w_q, w_kv, w_o,
           norm_ffn_g, w_up, ffn_dw_w, ffn_dw_b, w_down, norm_final_g):
    depth = w_in.shape[0]
    d = x.shape[-1]
    assert x.shape[1] % SEQ_TILE == 0 and SEQ_TILE % CONV_ROWS == 0
    assert conv_dw_w.shape[1] == CONV_WIDTH and ffn_dw_w.shape[1] == FFN_CONV_WIDTH
    assert CONV_WIDTH - 1 <= CONV_HALO and max(POOL_WINDOWS) - 1 <= POOL_HALO
    scale = float(d // XATTN_HEADS) ** -0.5
    assert math.frexp(scale)[0] == 0.5

    def row(v):
        return v.reshape(1, -1)

    gf = row(norm_final_g)
    for l in range(depth):
        k, v = _kv_call(mem, row(norm_mem_g[l]), w_kv[l].astype(BF16), scale)
        conv_w = jnp.broadcast_to(conv_dw_w[l][:, None, :],
                                  (CONV_WIDTH, SUBLANES, conv_dw_w.shape[-1]))
        mixer_w = (row(norm_mix_g[l]), w_in[l].astype(BF16), conv_w,
                   row(conv_dw_b[l]), row(conv_ln_g[l]), row(conv_ln_b[l]),
                   pool_w[l].astype(BF16), row(pool_scale[l]), w_out[l].astype(BF16))
        attn_w = (row(norm_xattn_g[l]), w_q[l].astype(BF16), w_o[l].astype(BF16))
        ffn_w = (row(norm_ffn_g[l]), w_up[l].astype(BF16), ffn_dw_w[l],
                 row(ffn_dw_b[l]), w_down[l].astype(BF16), gf)
        x = _layer_call(x, k, v, mixer_w, attn_w, ffn_w, final_norm=(l == depth - 1))
    return x
```

```python
import functools
import math

import jax
import jax.numpy as jnp
from jax import lax
from jax.experimental import pallas as pl
from jax.experimental.pallas import tpu as pltpu

EPS = 1e-6
CONV_WIDTH = 31
POOL_WINDOWS = (2, 4, 8, 16)
XATTN_HEADS = 4
FFN_CONV_WIDTH = 3

SUBLANES = 8
LANES = 128
MXU_COLS = 256
VMEM_LIMIT_BYTES = 60 * 1024 * 1024

SEQ_TILE = 512
CONV_HALO = 32
POOL_HALO = 16
FFN_HALO = 2
CONV_ROWS = 64
FFN_CHUNK = 2 * MXU_COLS
PIPE_DEPTH = 2
N_SLOTS = 2

BF16 = jnp.bfloat16
F32 = jnp.float32


def _rmsnorm(x, g):
    ms = jnp.mean(x * x, axis=-1, keepdims=True)
    return x * lax.rsqrt(ms + EPS) * g


def _dot(a, b):
    return jnp.dot(a, b, preferred_element_type=F32)


def _halo(tail, prev_tail):
    n, c = tail.shape
    shape3 = (n // SUBLANES, SUBLANES, c)
    cur = pltpu.roll(tail.reshape(shape3), 1, axis=1)
    prev = pltpu.roll(prev_tail.reshape(shape3), 1, axis=1)
    sub = lax.broadcasted_iota(jnp.int32, (1, SUBLANES, 1), 1)
    return jnp.where(sub >= 1, cur, prev).reshape(n, c)


def _window_sum(e, w):
    assert w & (w - 1) == 0
    tot, step = e, 1
    while step < w:
        rows = step * SUBLANES
        tot = tot + jnp.concatenate([tot[-rows:], tot[:-rows]], axis=0)
        step *= 2
    return tot


def _resident(shape):
    zeros = (0,) * len(shape)
    return pl.BlockSpec(shape, lambda *_: zeros, pipeline_mode=pl.Buffered(1))


def _params(n_axes):
    return pltpu.CompilerParams(
        dimension_semantics=("arbitrary",) * n_axes,
        vmem_limit_bytes=VMEM_LIMIT_BYTES)


def _kv_body(mem_ref, g_ref, wkv_ref, k_ref, v_ref, *, d_model, scale):
    mem_n = _rmsnorm(mem_ref[0], g_ref[...]).astype(BF16)
    kv = _dot(mem_n, wkv_ref[...])
    k_ref[0] = (kv[:, :d_model] * scale).astype(BF16)
    v_ref[0] = kv[:, d_model:].astype(BF16)


def _kv_call(mem, g, wkv, scale):
    b, n_mem, d = mem.shape
    return pl.pallas_call(
        functools.partial(_kv_body, d_model=d, scale=scale),
        grid=(b,),
        in_specs=[
            pl.BlockSpec((1, n_mem, d), lambda i: (i, 0, 0)),
            _resident((1, d)),
            _resident((d, 2 * d)),
        ],
        out_specs=[
            pl.BlockSpec((1, n_mem, d), lambda i: (i, 0, 0)),
            pl.BlockSpec((1, n_mem, d), lambda i: (i, 0, 0)),
        ],
        out_shape=[jax.ShapeDtypeStruct((b, n_mem, d), BF16)] * 2,
        compiler_params=_params(1),
        name="kv_proj",
    )(mem, g, wkv)


def _mixer_stage(x_tile, x1buf, seq_tile, g_ref, win_ref, dww_ref, dwb_ref, lng_ref,
                 lnb_ref, poolw_ref, pools_ref, wout_ref, conv_ext, conv_tail, pool_ext,
                 pool_tail, y_scr):
    ts = x1buf.shape[0]
    d_conv = conv_ext.shape[1]
    d_pool = pool_ext.shape[1]
    conv_halo = CONV_HALO * SUBLANES
    pool_halo = POOL_HALO * SUBLANES
    h = _rmsnorm(x_tile(), g_ref[...]).astype(BF16)
    u = _dot(h, win_ref[...])
    yield

    glu = u[:, :d_conv] * jax.nn.sigmoid(u[:, d_conv:2 * d_conv])
    conv_ext[conv_halo:, :] = glu
    conv_ext[0:conv_halo, :] = _halo(glu[ts - conv_halo:], conv_tail[...])
    conv_tail[...] = glu[ts - conv_halo:]
    pv = u[:, 2 * d_conv:]
    pool_ext[pool_halo:, :] = pv
    pool_ext[0:pool_halo, :] = _halo(pv[ts - pool_halo:], pool_tail[...])
    pool_tail[...] = pv[ts - pool_halo:]
    yield

    for r0 in range(0, ts, CONV_ROWS):
        acc = jnp.broadcast_to(dwb_ref[...], (CONV_ROWS, d_conv))
        acc = acc.reshape(CONV_ROWS // SUBLANES, SUBLANES, d_conv)
        for delay in range(CONV_WIDTH):
            start = conv_halo + r0 - delay * SUBLANES
            tap = conv_ext[start:start + CONV_ROWS, :]
            acc = acc + dww_ref[CONV_WIDTH - 1 - delay][None] * tap.reshape(acc.shape)
        acc = acc.reshape(CONV_ROWS, d_conv)
        mu = jnp.mean(acc, axis=-1, keepdims=True)
        cen = acc - mu
        var = jnp.mean(cen * cen, axis=-1, keepdims=True)
        yc = cen * lax.rsqrt(var + EPS) * lng_ref[...] + lnb_ref[...]
        y_scr[r0:r0 + CONV_ROWS, 0:d_conv] = (yc * jax.nn.sigmoid(yc)).astype(BF16)
        yield

    r = lax.broadcasted_iota(jnp.int32, (ts, 1), 0)
    sublane_bits = SUBLANES.bit_length() - 1
    token = (r & (SUBLANES - 1)) * (ts // SUBLANES) + (r >> sublane_bits)
    pos = (seq_tile * ts + token + 1).astype(F32)
    gd = d_pool // len(POOL_WINDOWS)
    for gi, w in enumerate(POOL_WINDOWS):
        cols = slice(gi * gd, (gi + 1) * gd)
        e = pool_ext[:, cols]
        inv_count = 1.0 / jnp.minimum(pos, float(w))
        pooled = _window_sum(e, w)[pool_halo:] * inv_count - e[pool_halo:]
        mixed = _dot(pooled.astype(BF16), poolw_ref[gi]) * pools_ref[:, cols]
        y_scr[:, d_conv + gi * gd:d_conv + (gi + 1) * gd] = mixed.astype(BF16)
    yield

    x1buf[...] = x_tile() + _dot(y_scr[...], wout_ref[...])
    yield


def _attn_stage(x1buf, x2buf, g_ref, wq_ref, k_ref, v_ref, wo_ref, o_scr):
    d = x1buf.shape[-1]
    hd = d // XATTN_HEADS
    h = _rmsnorm(x1buf[...], g_ref[...]).astype(BF16)
    q = _dot(h, wq_ref[...]).astype(BF16)
    yield
    for i in range(XATTN_HEADS):
        cols = slice(i * hd, (i + 1) * hd)
        sc = lax.dot_general(q[:, cols], k_ref[0, :, cols],
                             (((1,), (1,)), ((), ())),
                             preferred_element_type=F32)
        p = jnp.exp(sc - jnp.max(sc, axis=-1, keepdims=True))
        inv = 1.0 / jnp.sum(p, axis=-1, keepdims=True)
        o_scr[:, cols] = (_dot(p.astype(BF16), v_ref[0, :, cols]) * inv).astype(BF16)
        yield
    x2buf[...] = x1buf[...] + _dot(o_scr[...], wo_ref[...])
    yield


def _ffn_stage(x2buf, y_tile, g_ref, wup_ref, dww_ref, dwb_ref, wdown_ref, gf_ref, carry,
               act, final_norm):
    ts = x2buf.shape[0]
    d_ff = act.shape[1]
    halo = FFN_HALO * SUBLANES
    h = _rmsnorm(x2buf[...], g_ref[...]).astype(BF16)
    yield

    def conv(c0, width):
        cols = slice(c0, c0 + width)
        up = _dot(h, wup_ref[:, cols])
        tail = up[ts - halo:]
        ext = jnp.concatenate([_halo(tail, carry[:, cols]), up], axis=0)
        carry[:, cols] = tail
        out = dwb_ref[:, cols]
        for k in range(FFN_CONV_WIDTH):
            start = halo - (FFN_CONV_WIDTH - 1 - k) * SUBLANES
            out = out + dww_ref[k:k + 1, cols] * ext[start:start + ts]
        return out

    for c0 in range(0, d_ff, FFN_CHUNK):
        width = min(FFN_CHUNK, d_ff - c0)
        gate = conv(c0, width)
        val = conv(d_ff + c0, width)
        act[:, c0:c0 + width] = (gate * jax.nn.sigmoid(gate) * val).astype(BF16)
        yield

    y = x2buf[...] + _dot(act[...], wdown_ref[...])
    if final_norm:
        y = _rmsnorm(y, gf_ref[...])
    y_tile(y)
    yield


STAGE_ORDER = "MFAM" "FAFAMFAFAM" "FFMF" "MMMMM" "A" "M" "M"


def _tile_copies(hbm, buf, sems, slot, tile, tiles_per_row, to_hbm):
    chunk_len = buf.shape[1]
    row = tile // tiles_per_row
    first = (tile % tiles_per_row) * SUBLANES
    copies = []
    for c in range(SUBLANES):
        in_hbm = hbm.at[row, pl.ds((first + c) * chunk_len, chunk_len), :]
        in_vmem = buf.at[slot, :, c, :]
        src, dst = (in_vmem, in_hbm) if to_hbm else (in_hbm, in_vmem)
        copies.append(pltpu.make_async_copy(src, dst, sems.at[slot, c]))
    return copies


def _layer_body(x_hbm, k_ref, v_ref,
                gm_ref, win_ref, cdw_ref, cdb_ref, lng_ref, lnb_ref, poolw_ref,
                pools_ref, wout_ref,
                ga_ref, wq_ref, wo_ref,
                gff_ref, wup_ref, fdw_ref, fdb_ref, wdown_ref, gfin_ref,
                o_hbm,
                xp, yp, in_sems, out_sems, conv_ext, conv_tail, pool_ext, pool_tail,
                y_scr, o_scr, x1buf, x2buf, carry, act,
                *, tiles_per_row, n_tiles, final_norm):
    i = pl.program_id(0)
    seq_tile = i % tiles_per_row
    slot = i % N_SLOTS
    ts, d = x1buf.shape
    last_step = n_tiles + PIPE_DEPTH - 1

    def x_copies(tile, slot_):
        return _tile_copies(x_hbm, xp, in_sems, slot_, jnp.clip(tile, 0, n_tiles - 1),
                            tiles_per_row, to_hbm=False)

    def out_copies(tile, slot_):
        return _tile_copies(o_hbm, yp, out_sems, slot_, jnp.clip(tile, 0, n_tiles - 1),
                            tiles_per_row, to_hbm=True)

    @pl.when(i == 0)
    def _():
        for cp in x_copies(0, 0):
            cp.start()
        x1buf[...] = jnp.zeros(x1buf.shape, F32)
        x2buf[...] = jnp.zeros(x2buf.shape, F32)

    @pl.when(i < n_tiles)
    def _():
        for cp in x_copies(i, slot):
            cp.wait()

    @pl.when(i + 1 < n_tiles)
    def _():
        for cp in x_copies(i + 1, 1 - slot):
            cp.start()

    @pl.when(i >= PIPE_DEPTH + N_SLOTS)
    def _():
        for cp in out_copies(i - PIPE_DEPTH - N_SLOTS, slot):
            cp.wait()

    @pl.when(seq_tile == 0)
    def _():
        conv_tail[...] = jnp.zeros(conv_tail.shape, F32)
        pool_tail[...] = jnp.zeros(pool_tail.shape, F32)

    @pl.when(jnp.logical_or(i == 0, seq_tile == PIPE_DEPTH % tiles_per_row))
    def _():
        carry[...] = jnp.zeros(carry.shape, F32)

    def x_tile():
        return xp[slot].reshape(ts, d)

    def y_tile(y):
        yp[slot] = y.reshape(yp.shape[1:])

    stages = {
        "F": _ffn_stage(x2buf, y_tile, gff_ref, wup_ref, fdw_ref, fdb_ref, wdown_ref,
                        gfin_ref, carry, act, final_norm),
        "A": _attn_stage(x1buf, x2buf, ga_ref, wq_ref, k_ref, v_ref, wo_ref, o_scr),
        "M": _mixer_stage(x_tile, x1buf, seq_tile, gm_ref, win_ref, cdw_ref, cdb_ref,
                          lng_ref, lnb_ref, poolw_ref, pools_ref, wout_ref, conv_ext,
                          conv_tail, pool_ext, pool_tail, y_scr),
    }
    for name in STAGE_ORDER:
        next(stages[name])
    done = object()
    for gen in stages.values():
        assert next(gen, done) is done, "STAGE_ORDER does not cover every piece"

    @pl.when(i >= PIPE_DEPTH)
    def _():
        for cp in out_copies(i - PIPE_DEPTH, slot):
            cp.start()

    @pl.when(i == last_step)
    def _():
        for back in range(N_SLOTS - 1, -1, -1):
            for cp in out_copies(i - PIPE_DEPTH - back, (slot + N_SLOTS - back) % N_SLOTS):
                cp.wait()


def _layer_call(x, k, v, mixer_w, attn_w, ffn_w, final_norm):
    b, seq, d = x.shape
    n_mem = k.shape[1]
    ts = SEQ_TILE
    chunk_len = ts // SUBLANES
    tiles_per_row = seq // ts
    n_tiles = b * tiles_per_row
    d_conv = mixer_w[2].shape[-1]
    d_pool = mixer_w[7].shape[-1]
    d_ff = ffn_w[4].shape[0]

    def kv_map(i):
        return jnp.clip(i - 1, 0, n_tiles - 1) // tiles_per_row, 0, 0

    weights = (*mixer_w, *attn_w, *ffn_w)
    return pl.pallas_call(
        functools.partial(_layer_body, tiles_per_row=tiles_per_row, n_tiles=n_tiles,
                          final_norm=final_norm),
        grid=(n_tiles + PIPE_DEPTH,),
        in_specs=[pl.BlockSpec(memory_space=pl.ANY),
                  pl.BlockSpec((1, n_mem, d), kv_map),
                  pl.BlockSpec((1, n_mem, d), kv_map),
                  *[_resident(w.shape) for w in weights]],
        out_specs=pl.BlockSpec(memory_space=pl.ANY),
        out_shape=jax.ShapeDtypeStruct(x.shape, x.dtype),
        scratch_shapes=[pltpu.VMEM((N_SLOTS, chunk_len, SUBLANES, d), F32),
                        pltpu.VMEM((N_SLOTS, chunk_len, SUBLANES, d), F32),
                        pltpu.SemaphoreType.DMA((N_SLOTS, SUBLANES)),
                        pltpu.SemaphoreType.DMA((N_SLOTS, SUBLANES)),
                        pltpu.VMEM(((CONV_HALO * SUBLANES) + ts, d_conv), F32),
                        pltpu.VMEM((CONV_HALO * SUBLANES, d_conv), F32),
                        pltpu.VMEM(((POOL_HALO * SUBLANES) + ts, d_pool), F32),
                        pltpu.VMEM((POOL_HALO * SUBLANES, d_pool), F32),
                        pltpu.VMEM((ts, d_conv + d_pool), BF16),
                        pltpu.VMEM((ts, d), BF16),
                        pltpu.VMEM((ts, d), F32),
                        pltpu.VMEM((ts, d), F32),
                        pltpu.VMEM((FFN_HALO * SUBLANES, 2 * d_ff), F32),
                        pltpu.VMEM((ts, d_ff), BF16)],
        compiler_params=_params(1),
        name="layer",
    )(x, k, v, *weights)


def kernel(x, mem, norm_mix_g, w_in, conv_dw_w, conv_dw_b, conv_ln_g, conv_ln_b,
           pool_w, pool_scale, w_out, norm_xattn_g, norm_mem_g, w_q, w_kv, w_o,
           norm_ffn_g, w_up, ffn_dw_w, ffn_dw_b, w_down, norm_final_g):
    depth = w_in.shape[0]
    d = x.shape[-1]
    chunk_len = SEQ_TILE // SUBLANES
    assert x.shape[1] % SEQ_TILE == 0 and SEQ_TILE % CONV_ROWS == 0
    assert x.shape[0] * (x.shape[1] // SEQ_TILE) >= N_SLOTS
    assert conv_dw_w.shape[1] == CONV_WIDTH and ffn_dw_w.shape[1] == FFN_CONV_WIDTH
    assert CONV_WIDTH - 1 <= CONV_HALO <= chunk_len
    assert max(POOL_WINDOWS) - 1 <= POOL_HALO <= chunk_len
    assert FFN_CONV_WIDTH - 1 <= FFN_HALO <= chunk_len
    scale = float(d // XATTN_HEADS) ** -0.5
    assert math.frexp(scale)[0] == 0.5

    def row(v):
        return v.reshape(1, -1)

    gf = row(norm_final_g)
    for l in range(depth):
        k, v = _kv_call(mem, row(norm_mem_g[l]), w_kv[l].astype(BF16), scale)
        conv_w = jnp.broadcast_to(conv_dw_w[l][:, None, :],
                                  (CONV_WIDTH, SUBLANES, conv_dw_w.shape[-1]))
        mixer_w = (row(norm_mix_g[l]), w_in[l].astype(BF16), conv_w,
                   row(conv_dw_b[l]), row(conv_ln_g[l]), row(conv_ln_b[l]),
                   pool_w[l].astype(BF16), row(pool_scale[l]), w_out[l].astype(BF16))
        attn_w = (row(norm_xattn_g[l]), w_q[l].astype(BF16), w_o[l].astype(BF16))
        ffn_w = (row(norm_ffn_g[l]), w_up[l].astype(BF16), ffn_dw_w[l],
                 row(ffn_dw_b[l]), w_down[l].astype(BF16), gf)
        x = _layer_call(x, k, v, mixer_w, attn_w, ffn_w, final_norm=(l == depth - 1))
    return x
```

```python
import functools
import math

import jax
import jax.numpy as jnp
from jax import lax
from jax.experimental import pallas as pl
from jax.experimental.pallas import tpu as pltpu

EPS = 1e-6
CONV_WIDTH = 31
POOL_WINDOWS = (2, 4, 8, 16)
XATTN_HEADS = 4
FFN_CONV_WIDTH = 3

SUBLANES = 8
LANES = 128
MXU_COLS = 256
VMEM_LIMIT_BYTES = 60 * 1024 * 1024

SEQ_TILE = 512
CONV_HALO = 32
POOL_HALO = 16
FFN_HALO = 2
CONV_ROWS = 64
FFN_CHUNK = 2 * MXU_COLS
PIPE_DEPTH = 2
N_SLOTS = 2
STAGE_SLOTS = 2
STAGE_GROUPS = 64
STAGE_COLS = 1024

BF16 = jnp.bfloat16
F32 = jnp.float32


def _rmsnorm(x, g):
    ms = jnp.mean(x * x, axis=-1, keepdims=True)
    return x * lax.rsqrt(ms + EPS) * g


def _dot(a, b):
    return jnp.dot(a, b, preferred_element_type=F32)


def _halo(tail, prev_tail):
    n, c = tail.shape
    shape3 = (n // SUBLANES, SUBLANES, c)
    cur = pltpu.roll(tail.reshape(shape3), 1, axis=1)
    prev = pltpu.roll(prev_tail.reshape(shape3), 1, axis=1)
    sub = lax.broadcasted_iota(jnp.int32, (1, SUBLANES, 1), 1)
    return jnp.where(sub >= 1, cur, prev).reshape(n, c)


def _window_sum(e, w):
    assert w & (w - 1) == 0
    tot, step = e, 1
    while step < w:
        rows = step * SUBLANES
        tot = tot + jnp.concatenate([tot[-rows:], tot[:-rows]], axis=0)
        step *= 2
    return tot


def _resident(shape):
    zeros = (0,) * len(shape)
    return pl.BlockSpec(shape, lambda *_: zeros, pipeline_mode=pl.Buffered(1))


def _hbm():
    return pl.BlockSpec(memory_space=pl.ANY)


def _row_groups(w):
    k, n = w.shape
    return w.reshape(k // SUBLANES, SUBLANES, n)


def _load_weights_bf16(jobs, stage, sems):
    n_slots, groups, _, max_cols = stage.shape
    chunks = []
    for w_hbm, dst in jobs:
        kg, _, n = w_hbm.shape
        for g0 in range(0, kg, groups):
            for c0 in range(0, n, max_cols):
                chunks.append((w_hbm, dst, g0, min(groups, kg - g0), c0, min(max_cols, n - c0)))

    def copy(idx):
        w_hbm, _, g0, g, c0, c = chunks[idx]
        slot = idx % n_slots
        return pltpu.make_async_copy(w_hbm.at[pl.ds(g0, g), :, pl.ds(c0, c)],
                                     stage.at[slot, pl.ds(0, g), :, pl.ds(0, c)],
                                     sems.at[slot])

    ahead = n_slots - 1
    for idx in range(min(ahead, len(chunks))):
        copy(idx).start()
    for idx, (_, dst, g0, g, c0, c) in enumerate(chunks):
        copy(idx).wait()
        block = stage[idx % n_slots, 0:g, :, 0:c]
        dst[g0 * SUBLANES:(g0 + g) * SUBLANES, c0:c0 + c] = (
            block.reshape(g * SUBLANES, c).astype(BF16))
        if idx + ahead < len(chunks):
            copy(idx + ahead).start()


def _params(n_axes):
    return pltpu.CompilerParams(
        dimension_semantics=("arbitrary",) * n_axes,
        vmem_limit_bytes=VMEM_LIMIT_BYTES)


def _kv_body(mem_ref, g_ref, wkv_hbm, k_ref, v_ref, wkv, stage, sems, *, d_model, scale):
    @pl.when(pl.program_id(0) == 0)
    def _():
        _load_weights_bf16([(wkv_hbm, wkv)], stage, sems)

    mem_n = _rmsnorm(mem_ref[0], g_ref[...]).astype(BF16)
    kv = _dot(mem_n, wkv[...])
    k_ref[0] = (kv[:, :d_model] * scale).astype(BF16)
    v_ref[0] = kv[:, d_model:].astype(BF16)


def _kv_call(mem, g, wkv, scale):
    b, n_mem, d = mem.shape
    return pl.pallas_call(
        functools.partial(_kv_body, d_model=d, scale=scale),
        grid=(b,),
        in_specs=[
            pl.BlockSpec((1, n_mem, d), lambda i: (i, 0, 0)),
            _resident((1, d)),
            _hbm(),
        ],
        out_specs=[
            pl.BlockSpec((1, n_mem, d), lambda i: (i, 0, 0)),
            pl.BlockSpec((1, n_mem, d), lambda i: (i, 0, 0)),
        ],
        out_shape=[jax.ShapeDtypeStruct((b, n_mem, d), BF16)] * 2,
        scratch_shapes=[pltpu.VMEM(wkv.shape, BF16),
                        pltpu.VMEM((STAGE_SLOTS, STAGE_GROUPS, SUBLANES, STAGE_COLS), F32),
                        pltpu.SemaphoreType.DMA((STAGE_SLOTS,))],
        compiler_params=_params(1),
        name="kv_proj",
    )(mem, g, _row_groups(wkv))


def _mixer_stage(x_tile, x1buf, seq_tile, g_ref, win_ref, dww_ref, dwb_ref, lng_ref,
                 lnb_ref, poolw_ref, pools_ref, wout_ref, conv_ext, conv_tail, pool_ext,
                 pool_tail, y_scr):
    ts = x1buf.shape[0]
    d_conv = conv_ext.shape[1]
    d_pool = pool_ext.shape[1]
    conv_halo = CONV_HALO * SUBLANES
    pool_halo = POOL_HALO * SUBLANES
    h = _rmsnorm(x_tile(), g_ref[...]).astype(BF16)
    u = _dot(h, win_ref[...])
    yield

    glu = u[:, :d_conv] * jax.nn.sigmoid(u[:, d_conv:2 * d_conv])
    conv_ext[conv_halo:, :] = glu
    conv_ext[0:conv_halo, :] = _halo(glu[ts - conv_halo:], conv_tail[...])
    conv_tail[...] = glu[ts - conv_halo:]
    pv = u[:, 2 * d_conv:]
    pool_ext[pool_halo:, :] = pv
    pool_ext[0:pool_halo, :] = _halo(pv[ts - pool_halo:], pool_tail[...])
    pool_tail[...] = pv[ts - pool_halo:]
    yield

    for r0 in range(0, ts, CONV_ROWS):
        acc = jnp.broadcast_to(dwb_ref[...], (CONV_ROWS, d_conv))
        acc = acc.reshape(CONV_ROWS // SUBLANES, SUBLANES, d_conv)
        for delay in range(CONV_WIDTH):
            start = conv_halo + r0 - delay * SUBLANES
            tap = conv_ext[start:start + CONV_ROWS, :]
            acc = acc + dww_ref[CONV_WIDTH - 1 - delay][None] * tap.reshape(acc.shape)
        acc = acc.reshape(CONV_ROWS, d_conv)
        mu = jnp.mean(acc, axis=-1, keepdims=True)
        cen = acc - mu
        var = jnp.mean(cen * cen, axis=-1, keepdims=True)
        yc = cen * lax.rsqrt(var + EPS) * lng_ref[...] + lnb_ref[...]
        y_scr[r0:r0 + CONV_ROWS, 0:d_conv] = (yc * jax.nn.sigmoid(yc)).astype(BF16)
        yield

    r = lax.broadcasted_iota(jnp.int32, (ts, 1), 0)
    sublane_bits = SUBLANES.bit_length() - 1
    token = (r & (SUBLANES - 1)) * (ts // SUBLANES) + (r >> sublane_bits)
    pos = (seq_tile * ts + token + 1).astype(F32)
    gd = d_pool // len(POOL_WINDOWS)
    for gi, w in enumerate(POOL_WINDOWS):
        cols = slice(gi * gd, (gi + 1) * gd)
        e = pool_ext[:, cols]
        inv_count = 1.0 / jnp.minimum(pos, float(w))
        pooled = _window_sum(e, w)[pool_halo:] * inv_count - e[pool_halo:]
        mixed = _dot(pooled.astype(BF16), poolw_ref[gi].astype(BF16)) * pools_ref[:, cols]
        y_scr[:, d_conv + gi * gd:d_conv + (gi + 1) * gd] = mixed.astype(BF16)
    yield

    x1buf[...] = x_tile() + _dot(y_scr[...], wout_ref[...])
    yield


def _attn_stage(x1buf, x2buf, g_ref, wq_ref, k_ref, v_ref, wo_ref, o_scr):
    d = x1buf.shape[-1]
    hd = d // XATTN_HEADS
    h = _rmsnorm(x1buf[...], g_ref[...]).astype(BF16)
    q = _dot(h, wq_ref[...]).astype(BF16)
    yield
    for i in range(XATTN_HEADS):
        cols = slice(i * hd, (i + 1) * hd)
        sc = lax.dot_general(q[:, cols], k_ref[0, :, cols],
                             (((1,), (1,)), ((), ())),
                             preferred_element_type=F32)
        p = jnp.exp(sc - jnp.max(sc, axis=-1, keepdims=True))
        inv = 1.0 / jnp.sum(p, axis=-1, keepdims=True)
        o_scr[:, cols] = (_dot(p.astype(BF16), v_ref[0, :, cols]) * inv).astype(BF16)
        yield
    x2buf[...] = x1buf[...] + _dot(o_scr[...], wo_ref[...])
    yield


def _ffn_stage(x2buf, y_tile, g_ref, wup_ref, dww_ref, dwb_ref, wdown_ref, gf_ref, carry,
               act, final_norm):
    ts = x2buf.shape[0]
    d_ff = act.shape[1]
    halo = FFN_HALO * SUBLANES
    h = _rmsnorm(x2buf[...], g_ref[...]).astype(BF16)
    yield

    def conv(c0, width):
        cols = slice(c0, c0 + width)
        up = _dot(h, wup_ref[:, cols])
        tail = up[ts - halo:]
        ext = jnp.concatenate([_halo(tail, carry[:, cols]), up], axis=0)
        carry[:, cols] = tail
        out = dwb_ref[:, cols]
        for k in range(FFN_CONV_WIDTH):
            start = halo - (FFN_CONV_WIDTH - 1 - k) * SUBLANES
            out = out + dww_ref[k:k + 1, cols] * ext[start:start + ts]
        return out

    for c0 in range(0, d_ff, FFN_CHUNK):
        width = min(FFN_CHUNK, d_ff - c0)
        gate = conv(c0, width)
        val = conv(d_ff + c0, width)
        act[:, c0:c0 + width] = (gate * jax.nn.sigmoid(gate) * val).astype(BF16)
        yield

    y = x2buf[...] + _dot(act[...], wdown_ref[...])
    if final_norm:
        y = _rmsnorm(y, gf_ref[...])
    y_tile(y)
    yield


STAGE_ORDER = "MFAM" "FMFMFMFMFMF" "F" "AAAA" "MMM" "A" "M" "M"


def _tile_copies(hbm, buf, sems, slot, tile, tiles_per_row, to_hbm):
    chunk_len = buf.shape[1]
    row = tile // tiles_per_row
    first = (tile % tiles_per_row) * SUBLANES
    copies = []
    for c in range(SUBLANES):
        in_hbm = hbm.at[row, pl.ds((first + c) * chunk_len, chunk_len), :]
        in_vmem = buf.at[slot, :, c, :]
        src, dst = (in_vmem, in_hbm) if to_hbm else (in_hbm, in_vmem)
        copies.append(pltpu.make_async_copy(src, dst, sems.at[slot, c]))
    return copies


def _layer_body(x_hbm, k_ref, v_ref,
                gm_ref, win_hbm, cdw_ref, cdb_ref, lng_ref, lnb_ref, poolw_ref,
                pools_ref, wout_hbm,
                ga_ref, wq_hbm, wo_hbm,
                gff_ref, wup_hbm, fdw_ref, fdb_ref, wdown_hbm, gfin_ref,
                o_hbm,
                xp, yp, in_sems, out_sems, conv_ext, conv_tail, pool_ext, pool_tail,
                y_scr, o_scr, x1buf, x2buf, carry, act,
                win, wout, wq, wo, wup, wdown, cdw_rep, stage, w_sems,
                *, tiles_per_row, n_tiles, final_norm):
    i = pl.program_id(0)
    seq_tile = i % tiles_per_row
    slot = i % N_SLOTS
    ts, d = x1buf.shape
    last_step = n_tiles + PIPE_DEPTH - 1

    def x_copies(tile, slot_):
        return _tile_copies(x_hbm, xp, in_sems, slot_, jnp.clip(tile, 0, n_tiles - 1),
                            tiles_per_row, to_hbm=False)

    def out_copies(tile, slot_):
        return _tile_copies(o_hbm, yp, out_sems, slot_, jnp.clip(tile, 0, n_tiles - 1),
                            tiles_per_row, to_hbm=True)

    @pl.when(i == 0)
    def _():
        for cp in x_copies(0, 0):
            cp.start()
        x1buf[...] = jnp.zeros(x1buf.shape, F32)
        x2buf[...] = jnp.zeros(x2buf.shape, F32)
        _load_weights_bf16([(win_hbm, win), (wq_hbm, wq), (wup_hbm, wup), (wout_hbm, wout),
                            (wo_hbm, wo), (wdown_hbm, wdown)], stage, w_sems)
        cdw_rep[...] = jnp.broadcast_to(cdw_ref[...][:, None, :], cdw_rep.shape)

    @pl.when(i < n_tiles)
    def _():
        for cp in x_copies(i, slot):
            cp.wait()

    @pl.when(i + 1 < n_tiles)
    def _():
        for cp in x_copies(i + 1, 1 - slot):
            cp.start()

    @pl.when(i >= PIPE_DEPTH + N_SLOTS)
    def _():
        for cp in out_copies(i - PIPE_DEPTH - N_SLOTS, slot):
            cp.wait()

    @pl.when(seq_tile == 0)
    def _():
        conv_tail[...] = jnp.zeros(conv_tail.shape, F32)
        pool_tail[...] = jnp.zeros(pool_tail.shape, F32)

    @pl.when(jnp.logical_or(i == 0, seq_tile == PIPE_DEPTH % tiles_per_row))
    def _():
        carry[...] = jnp.zeros(carry.shape, F32)

    def x_tile():
        return xp[slot].reshape(ts, d)

    def y_tile(y):
        yp[slot] = y.reshape(yp.shape[1:])

    stages = {
        "F": _ffn_stage(x2buf, y_tile, gff_ref, wup, fdw_ref, fdb_ref, wdown, gfin_ref,
                        carry, act, final_norm),
        "A": _attn_stage(x1buf, x2buf, ga_ref, wq, k_ref, v_ref, wo, o_scr),
        "M": _mixer_stage(x_tile, x1buf, seq_tile, gm_ref, win, cdw_rep, cdb_ref,
                          lng_ref, lnb_ref, poolw_ref, pools_ref, wout, conv_ext,
                          conv_tail, pool_ext, pool_tail, y_scr),
    }
    for name in STAGE_ORDER:
        next(stages[name])
    done = object()
    for gen in stages.values():
        assert next(gen, done) is done, "STAGE_ORDER does not cover every piece"

    @pl.when(i >= PIPE_DEPTH)
    def _():
        for cp in out_copies(i - PIPE_DEPTH, slot):
            cp.start()

    @pl.when(i == last_step)
    def _():
        for back in range(N_SLOTS - 1, -1, -1):
            for cp in out_copies(i - PIPE_DEPTH - back, (slot + N_SLOTS - back) % N_SLOTS):
                cp.wait()


def _layer_call(x, k, v, mixer_w, attn_w, ffn_w, final_norm):
    b, seq, d = x.shape
    n_mem = k.shape[1]
    ts = SEQ_TILE
    chunk_len = ts // SUBLANES
    tiles_per_row = seq // ts
    n_tiles = b * tiles_per_row
    d_conv = mixer_w[2].shape[-1]
    d_pool = mixer_w[7].shape[-1]
    d_ff = ffn_w[4].shape[0] * SUBLANES

    def kv_map(i):
        return jnp.clip(i - 1, 0, n_tiles - 1) // tiles_per_row, 0, 0

    weights = (*mixer_w, *attn_w, *ffn_w)
    matmul_w = (mixer_w[1], mixer_w[8], attn_w[1], attn_w[2], ffn_w[1], ffn_w[4])
    weight_specs = [_hbm() if any(w is m for m in matmul_w) else _resident(w.shape)
                    for w in weights]
    matmul_shapes = [(w.shape[0] * SUBLANES, w.shape[2]) for w in matmul_w]
    return pl.pallas_call(
        functools.partial(_layer_body, tiles_per_row=tiles_per_row, n_tiles=n_tiles,
                          final_norm=final_norm),
        grid=(n_tiles + PIPE_DEPTH,),
        in_specs=[pl.BlockSpec(memory_space=pl.ANY),
                  pl.BlockSpec((1, n_mem, d), kv_map),
                  pl.BlockSpec((1, n_mem, d), kv_map),
                  *weight_specs],
        out_specs=pl.BlockSpec(memory_space=pl.ANY),
        out_shape=jax.ShapeDtypeStruct(x.shape, x.dtype),
        scratch_shapes=[pltpu.VMEM((N_SLOTS, chunk_len, SUBLANES, d), F32),
                        pltpu.VMEM((N_SLOTS, chunk_len, SUBLANES, d), F32),
                        pltpu.SemaphoreType.DMA((N_SLOTS, SUBLANES)),
                        pltpu.SemaphoreType.DMA((N_SLOTS, SUBLANES)),
                        pltpu.VMEM(((CONV_HALO * SUBLANES) + ts, d_conv), F32),
                        pltpu.VMEM((CONV_HALO * SUBLANES, d_conv), F32),
                        pltpu.VMEM(((POOL_HALO * SUBLANES) + ts, d_pool), F32),
                        pltpu.VMEM((POOL_HALO * SUBLANES, d_pool), F32),
                        pltpu.VMEM((ts, d_conv + d_pool), BF16),
                        pltpu.VMEM((ts, d), BF16),
                        pltpu.VMEM((ts, d), F32),
                        pltpu.VMEM((ts, d), F32),
                        pltpu.VMEM((FFN_HALO * SUBLANES, 2 * d_ff), F32),
                        pltpu.VMEM((ts, d_ff), BF16),
                        *[pltpu.VMEM(s, BF16) for s in matmul_shapes],
                        pltpu.VMEM((CONV_WIDTH, SUBLANES, d_conv), F32),
                        pltpu.VMEM((STAGE_SLOTS, STAGE_GROUPS, SUBLANES, STAGE_COLS), F32),
                        pltpu.SemaphoreType.DMA((STAGE_SLOTS,))],
        compiler_params=_params(1),
        name="layer",
    )(x, k, v, *weights)


def kernel(x, mem, norm_mix_g, w_in, conv_dw_w, conv_dw_b, conv_ln_g, conv_ln_b,
           pool_w, pool_scale, w_out, norm_xattn_g, norm_mem_g, w_q, w_kv, w_o,
           norm_ffn_g, w_up, ffn_dw_w, ffn_dw_b, w_down, norm_final_g):
    depth = w_in.shape[0]
    d = x.shape[-1]
    chunk_len = SEQ_TILE // SUBLANES
    assert x.shape[1] % SEQ_TILE == 0 and SEQ_TILE % CONV_ROWS == 0
    assert x.shape[0] * (x.shape[1] // SEQ_TILE) >= N_SLOTS
    assert conv_dw_w.shape[1] == CONV_WIDTH and ffn_dw_w.shape[1] == FFN_CONV_WIDTH
    assert CONV_WIDTH - 1 <= CONV_HALO <= chunk_len
    assert max(POOL_WINDOWS) - 1 <= POOL_HALO <= chunk_len
    assert FFN_CONV_WIDTH - 1 <= FFN_HALO <= chunk_len
    scale = float(d // XATTN_HEADS) ** -0.5
    assert math.frexp(scale)[0] == 0.5

    def row(v):
        return v.reshape(1, -1)

    gf = row(norm_final_g)
    for l in range(depth):
        k, v = _kv_call(mem, row(norm_mem_g[l]), w_kv[l], scale)
        mixer_w = (row(norm_mix_g[l]), _row_groups(w_in[l]), conv_dw_w[l],
                   row(conv_dw_b[l]), row(conv_ln_g[l]), row(conv_ln_b[l]),
                   pool_w[l], row(pool_scale[l]), _row_groups(w_out[l]))
        attn_w = (row(norm_xattn_g[l]), _row_groups(w_q[l]), _row_groups(w_o[l]))
        ffn_w = (row(norm_ffn_g[l]), _row_groups(w_up[l]), ffn_dw_w[l],
                 row(ffn_dw_b[l]), _row_groups(w_down[l]), gf)
        x = _layer_call(x, k, v, mixer_w, attn_w, ffn_w, final_norm=(l == depth - 1))
    return x
```

```python
import functools
import math

import jax
import jax.numpy as jnp
from jax import lax
from jax.experimental import pallas as pl
from jax.experimental.pallas import tpu as pltpu

EPS = 1e-6
CONV_WIDTH = 31
POOL_WINDOWS = (2, 4, 8, 16)
XATTN_HEADS = 4
FFN_CONV_WIDTH = 3

SUBLANES = 8
LANES = 128
MXU_COLS = 256
VMEM_LIMIT_BYTES = 60 * 1024 * 1024

SEQ_TILE = 512
CONV_HALO = 32
POOL_HALO = 16
FFN_HALO = 2
CONV_ROWS = 64
FFN_CHUNK = 2 * MXU_COLS
PIPE_DEPTH = 2
N_SLOTS = 2
STAGE_SLOTS = 2
KV_STAGE_SLOTS = 4
KV_ROWS = 2
STAGE_GROUPS = 64
STAGE_COLS = 1024

BF16 = jnp.bfloat16
F32 = jnp.float32


def _rmsnorm(x, g):
    ms = jnp.mean(x * x, axis=-1, keepdims=True)
    return x * lax.rsqrt(ms + EPS) * g


def _dot(a, b):
    return jnp.dot(a, b, preferred_element_type=F32)


def _halo(tail, prev_tail):
    n, c = tail.shape
    shape3 = (n // SUBLANES, SUBLANES, c)
    cur = pltpu.roll(tail.reshape(shape3), 1, axis=1)
    prev = pltpu.roll(prev_tail.reshape(shape3), 1, axis=1)
    sub = lax.broadcasted_iota(jnp.int32, (1, SUBLANES, 1), 1)
    return jnp.where(sub >= 1, cur, prev).reshape(n, c)


def _window_sum(e, w):
    assert w & (w - 1) == 0
    tot, step = e, 1
    while step < w:
        rows = step * SUBLANES
        tot = tot + jnp.concatenate([tot[-rows:], tot[:-rows]], axis=0)
        step *= 2
    return tot


def _resident(shape):
    zeros = (0,) * len(shape)
    return pl.BlockSpec(shape, lambda *_: zeros, pipeline_mode=pl.Buffered(1))


def _hbm():
    return pl.BlockSpec(memory_space=pl.ANY)


def _row_groups(w):
    k, n = w.shape
    return w.reshape(k // SUBLANES, SUBLANES, n)


def _load_weights_bf16(jobs, slots, sems):
    groups, _, max_cols = slots[0].shape
    chunks = []
    for w_hbm, dst in jobs:
        kg, _, n = w_hbm.shape
        for g0 in range(0, kg, groups):
            for c0 in range(0, n, max_cols):
                chunks.append((w_hbm, dst, g0, min(groups, kg - g0), c0, min(max_cols, n - c0)))

    def copy(idx):
        w_hbm, _, g0, g, c0, c = chunks[idx]
        slot = idx % len(slots)
        return pltpu.make_async_copy(w_hbm.at[pl.ds(g0, g), :, pl.ds(c0, c)],
                                     slots[slot].at[pl.ds(0, g), :, pl.ds(0, c)],
                                     sems.at[slot])

    ahead = len(slots) - 1
    for idx in range(min(ahead, len(chunks))):
        copy(idx).start()
    for idx, (_, dst, g0, g, c0, c) in enumerate(chunks):
        copy(idx).wait()
        block = slots[idx % len(slots)][0:g, :, 0:c]
        dst[g0 * SUBLANES:(g0 + g) * SUBLANES, c0:c0 + c] = (
            block.reshape(g * SUBLANES, c).astype(BF16))
        if idx + ahead < len(chunks):
            copy(idx + ahead).start()


def _params(n_axes):
    return pltpu.CompilerParams(
        dimension_semantics=("arbitrary",) * n_axes,
        vmem_limit_bytes=VMEM_LIMIT_BYTES)


def _kv_body(mem_ref, g_ref, wkv_hbm, k_ref, v_ref, wkv, stage, sems, *, d_model, scale):
    @pl.when(pl.program_id(0) == 0)
    def _():
        _load_weights_bf16([(wkv_hbm, wkv)], [stage.at[s] for s in range(stage.shape[0])],
                           sems)

    rows, n_mem, _ = mem_ref.shape
    mem = mem_ref[...].reshape(rows * n_mem, d_model)
    kv = _dot(_rmsnorm(mem, g_ref[...]).astype(BF16), wkv[...])
    k_ref[...] = (kv[:, :d_model] * scale).astype(BF16).reshape(k_ref.shape)
    v_ref[...] = kv[:, d_model:].astype(BF16).reshape(v_ref.shape)


def _kv_call(mem, g, wkv, scale):
    b, n_mem, d = mem.shape
    rows = KV_ROWS if b % KV_ROWS == 0 else 1
    block = pl.BlockSpec((rows, n_mem, d), lambda i: (i, 0, 0))
    return pl.pallas_call(
        functools.partial(_kv_body, d_model=d, scale=scale),
        grid=(b // rows,),
        in_specs=[block, _resident((1, d)), _hbm()],
        out_specs=[block, block],
        out_shape=[jax.ShapeDtypeStruct((b, n_mem, d), BF16)] * 2,
        scratch_shapes=[pltpu.VMEM(wkv.shape, BF16),
                        pltpu.VMEM((KV_STAGE_SLOTS, STAGE_GROUPS, SUBLANES, STAGE_COLS), F32),
                        pltpu.SemaphoreType.DMA((KV_STAGE_SLOTS,))],
        compiler_params=_params(1),
        name="kv_proj",
    )(mem, g, _row_groups(wkv))


def _mixer_stage(x_tile, x1buf, seq_tile, g_ref, win_ref, dww_ref, dwb_ref, lng_ref,
                 lnb_ref, poolw_ref, pools_ref, wout_ref, conv_ext, conv_tail, pool_ext,
                 pool_tail, y_scr):
    ts = x1buf.shape[0]
    d_conv = conv_ext.shape[1]
    d_pool = pool_ext.shape[1]
    conv_halo = CONV_HALO * SUBLANES
    pool_halo = POOL_HALO * SUBLANES
    h = _rmsnorm(x_tile(), g_ref[...]).astype(BF16)
    u = _dot(h, win_ref[...])
    yield

    glu = u[:, :d_conv] * jax.nn.sigmoid(u[:, d_conv:2 * d_conv])
    conv_ext[conv_halo:, :] = glu
    conv_ext[0:conv_halo, :] = _halo(glu[ts - conv_halo:], conv_tail[...])
    conv_tail[...] = glu[ts - conv_halo:]
    pv = u[:, 2 * d_conv:]
    pool_ext[pool_halo:, :] = pv
    pool_ext[0:pool_halo, :] = _halo(pv[ts - pool_halo:], pool_tail[...])
    pool_tail[...] = pv[ts - pool_halo:]
    yield

    for r0 in range(0, ts, CONV_ROWS):
        acc = jnp.broadcast_to(dwb_ref[...], (CONV_ROWS, d_conv))
        acc = acc.reshape(CONV_ROWS // SUBLANES, SUBLANES, d_conv)
        for delay in range(CONV_WIDTH):
            start = conv_halo + r0 - delay * SUBLANES
            tap = conv_ext[start:start + CONV_ROWS, :]
            acc = acc + dww_ref[CONV_WIDTH - 1 - delay][None] * tap.reshape(acc.shape)
        acc = acc.reshape(CONV_ROWS, d_conv)
        mu = jnp.mean(acc, axis=-1, keepdims=True)
        cen = acc - mu
        var = jnp.mean(cen * cen, axis=-1, keepdims=True)
        yc = cen * lax.rsqrt(var + EPS) * lng_ref[...] + lnb_ref[...]
        y_scr[r0:r0 + CONV_ROWS, 0:d_conv] = (yc * jax.nn.sigmoid(yc)).astype(BF16)
        yield

    r = lax.broadcasted_iota(jnp.int32, (ts, 1), 0)
    sublane_bits = SUBLANES.bit_length() - 1
    token = (r & (SUBLANES - 1)) * (ts // SUBLANES) + (r >> sublane_bits)
    pos = (seq_tile * ts + token + 1).astype(F32)
    gd = d_pool // len(POOL_WINDOWS)
    for gi, w in enumerate(POOL_WINDOWS):
        cols = slice(gi * gd, (gi + 1) * gd)
        e = pool_ext[:, cols]
        inv_count = 1.0 / jnp.minimum(pos, float(w))
        pooled = _window_sum(e, w)[pool_halo:] * inv_count - e[pool_halo:]
        mixed = _dot(pooled.astype(BF16), poolw_ref[gi].astype(BF16)) * pools_ref[:, cols]
        y_scr[:, d_conv + gi * gd:d_conv + (gi + 1) * gd] = mixed.astype(BF16)
    yield

    x1buf[...] = x_tile() + _dot(y_scr[...], wout_ref[...])
    yield


def _attn_stage(x1buf, x2buf, g_ref, wq_ref, k_ref, v_ref, wo_ref, o_scr):
    d = x1buf.shape[-1]
    hd = d // XATTN_HEADS
    h = _rmsnorm(x1buf[...], g_ref[...]).astype(BF16)
    q = _dot(h, wq_ref[...]).astype(BF16)
    yield
    for i in range(XATTN_HEADS):
        cols = slice(i * hd, (i + 1) * hd)
        sc = lax.dot_general(q[:, cols], k_ref[0, :, cols],
                             (((1,), (1,)), ((), ())),
                             preferred_element_type=F32)
        p = jnp.exp(sc - jnp.max(sc, axis=-1, keepdims=True))
        inv = 1.0 / jnp.sum(p, axis=-1, keepdims=True)
        o_scr[:, cols] = (_dot(p.astype(BF16), v_ref[0, :, cols]) * inv).astype(BF16)
        yield
    x2buf[...] = x1buf[...] + _dot(o_scr[...], wo_ref[...])
    yield


def _ffn_stage(x2buf, y_tile, g_ref, wup_ref, dww_ref, dwb_ref, wdown_ref, gf_ref, carry,
               act, final_norm):
    ts = x2buf.shape[0]
    d_ff = act.shape[1]
    halo = FFN_HALO * SUBLANES
    h = _rmsnorm(x2buf[...], g_ref[...]).astype(BF16)
    yield

    def conv(c0, width):
        cols = slice(c0, c0 + width)
        up = _dot(h, wup_ref[:, cols])
        tail = up[ts - halo:]
        ext = jnp.concatenate([_halo(tail, carry[:, cols]), up], axis=0)
        carry[:, cols] = tail
        out = dwb_ref[:, cols]
        for k in range(FFN_CONV_WIDTH):
            start = halo - (FFN_CONV_WIDTH - 1 - k) * SUBLANES
            out = out + dww_ref[k:k + 1, cols] * ext[start:start + ts]
        return out

    for c0 in range(0, d_ff, FFN_CHUNK):
        width = min(FFN_CHUNK, d_ff - c0)
        gate = conv(c0, width)
        val = conv(d_ff + c0, width)
        act[:, c0:c0 + width] = (gate * jax.nn.sigmoid(gate) * val).astype(BF16)
        yield

    y = x2buf[...] + _dot(act[...], wdown_ref[...])
    if final_norm:
        y = _rmsnorm(y, gf_ref[...])
    y_tile(y)
    yield


STAGE_ORDER = "MFAM" "FMFMFMFMFMF" "F" "AAAA" "MMM" "A" "M" "M"


def _tile_copies(hbm, buf, sems, slot, tile, tiles_per_row, to_hbm):
    chunk_len = buf.shape[1]
    row = tile // tiles_per_row
    first = (tile % tiles_per_row) * SUBLANES
    copies = []
    for c in range(SUBLANES):
        in_hbm = hbm.at[row, pl.ds((first + c) * chunk_len, chunk_len), :]
        in_vmem = buf.at[slot, :, c, :]
        src, dst = (in_vmem, in_hbm) if to_hbm else (in_hbm, in_vmem)
        copies.append(pltpu.make_async_copy(src, dst, sems.at[slot, c]))
    return copies


def _layer_body(x_hbm, k_ref, v_ref,
                gm_ref, win_hbm, cdw_ref, cdb_ref, lng_ref, lnb_ref, poolw_ref,
                pools_ref, wout_hbm,
                ga_ref, wq_hbm, wo_hbm,
                gff_ref, wup_hbm, fdw_ref, fdb_ref, wdown_hbm, gfin_ref,
                o_hbm,
                xp, yp, in_sems, out_sems, conv_ext, conv_tail, pool_ext, pool_tail,
                y_scr, o_scr, x1buf, x2buf, carry, act,
                win, wout, wq, wo, wup, wdown, cdw_rep, stage, w_sems,
                *, tiles_per_row, n_tiles, final_norm):
    i = pl.program_id(0)
    seq_tile = i % tiles_per_row
    slot = i % N_SLOTS
    ts, d = x1buf.shape
    last_step = n_tiles + PIPE_DEPTH - 1

    def x_copies(tile, slot_):
        return _tile_copies(x_hbm, xp, in_sems, slot_, jnp.clip(tile, 0, n_tiles - 1),
                            tiles_per_row, to_hbm=False)

    def out_copies(tile, slot_):
        return _tile_copies(o_hbm, yp, out_sems, slot_, jnp.clip(tile, 0, n_tiles - 1),
                            tiles_per_row, to_hbm=True)

    @pl.when(i == 0)
    def _():
        for cp in x_copies(0, 0):
            cp.start()
        x1buf[...] = jnp.zeros(x1buf.shape, F32)
        x2buf[...] = jnp.zeros(x2buf.shape, F32)
        slots = [stage.at[s] for s in range(stage.shape[0])]
        if yp.shape[1:] == stage.shape[1:]:
            slots += [yp.at[s] for s in range(yp.shape[0])] + [xp.at[1]]
        assert len(slots) <= w_sems.shape[0]
        _load_weights_bf16([(win_hbm, win), (wq_hbm, wq), (wup_hbm, wup), (wout_hbm, wout),
                            (wo_hbm, wo), (wdown_hbm, wdown)], slots, w_sems)
        cdw_rep[...] = jnp.broadcast_to(cdw_ref[...][:, None, :], cdw_rep.shape)

    @pl.when(i < n_tiles)
    def _():
        for cp in x_copies(i, slot):
            cp.wait()

    @pl.when(i + 1 < n_tiles)
    def _():
        for cp in x_copies(i + 1, 1 - slot):
            cp.start()

    @pl.when(i >= PIPE_DEPTH + N_SLOTS)
    def _():
        for cp in out_copies(i - PIPE_DEPTH - N_SLOTS, slot):
            cp.wait()

    @pl.when(seq_tile == 0)
    def _():
        conv_tail[...] = jnp.zeros(conv_tail.shape, F32)
        pool_tail[...] = jnp.zeros(pool_tail.shape, F32)

    @pl.when(jnp.logical_or(i == 0, seq_tile == PIPE_DEPTH % tiles_per_row))
    def _():
        carry[...] = jnp.zeros(carry.shape, F32)

    def x_tile():
        return xp[slot].reshape(ts, d)

    def y_tile(y):
        yp[slot] = y.reshape(yp.shape[1:])

    stages = {
        "F": _ffn_stage(x2buf, y_tile, gff_ref, wup, fdw_ref, fdb_ref, wdown, gfin_ref,
                        carry, act, final_norm),
        "A": _attn_stage(x1buf, x2buf, ga_ref, wq, k_ref, v_ref, wo, o_scr),
        "M": _mixer_stage(x_tile, x1buf, seq_tile, gm_ref, win, cdw_rep, cdb_ref,
                          lng_ref, lnb_ref, poolw_ref, pools_ref, wout, conv_ext,
                          conv_tail, pool_ext, pool_tail, y_scr),
    }
    for name in STAGE_ORDER:
        next(stages[name])
    done = object()
    for gen in stages.values():
        assert next(gen, done) is done, "STAGE_ORDER does not cover every piece"

    @pl.when(i >= PIPE_DEPTH)
    def _():
        for cp in out_copies(i - PIPE_DEPTH, slot):
            cp.start()

    @pl.when(i == last_step)
    def _():
        for back in range(N_SLOTS - 1, -1, -1):
            for cp in out_copies(i - PIPE_DEPTH - back, (slot + N_SLOTS - back) % N_SLOTS):
                cp.wait()


def _layer_call(x, k, v, mixer_w, attn_w, ffn_w, final_norm):
    b, seq, d = x.shape
    n_mem = k.shape[1]
    ts = SEQ_TILE
    chunk_len = ts // SUBLANES
    tiles_per_row = seq // ts
    n_tiles = b * tiles_per_row
    d_conv = mixer_w[2].shape[-1]
    d_pool = mixer_w[7].shape[-1]
    d_ff = ffn_w[4].shape[0] * SUBLANES

    def kv_map(i):
        return jnp.clip(i - 1, 0, n_tiles - 1) // tiles_per_row, 0, 0

    weights = (*mixer_w, *attn_w, *ffn_w)
    matmul_w = (mixer_w[1], mixer_w[8], attn_w[1], attn_w[2], ffn_w[1], ffn_w[4])
    weight_specs = [_hbm() if any(w is m for m in matmul_w) else _resident(w.shape)
                    for w in weights]
    matmul_shapes = [(w.shape[0] * SUBLANES, w.shape[2]) for w in matmul_w]
    return pl.pallas_call(
        functools.partial(_layer_body, tiles_per_row=tiles_per_row, n_tiles=n_tiles,
                          final_norm=final_norm),
        grid=(n_tiles + PIPE_DEPTH,),
        in_specs=[pl.BlockSpec(memory_space=pl.ANY),
                  pl.BlockSpec((1, n_mem, d), kv_map),
                  pl.BlockSpec((1, n_mem, d), kv_map),
                  *weight_specs],
        out_specs=pl.BlockSpec(memory_space=pl.ANY),
        out_shape=jax.ShapeDtypeStruct(x.shape, x.dtype),
        scratch_shapes=[pltpu.VMEM((N_SLOTS, chunk_len, SUBLANES, d), F32),
                        pltpu.VMEM((N_SLOTS, chunk_len, SUBLANES, d), F32),
                        pltpu.SemaphoreType.DMA((N_SLOTS, SUBLANES)),
                        pltpu.SemaphoreType.DMA((N_SLOTS, SUBLANES)),
                        pltpu.VMEM(((CONV_HALO * SUBLANES) + ts, d_conv), F32),
                        pltpu.VMEM((CONV_HALO * SUBLANES, d_conv), F32),
                        pltpu.VMEM(((POOL_HALO * SUBLANES) + ts, d_pool), F32),
                        pltpu.VMEM((POOL_HALO * SUBLANES, d_pool), F32),
                        pltpu.VMEM((ts, d_conv + d_pool), BF16),
                        pltpu.VMEM((ts, d), BF16),
                        pltpu.VMEM((ts, d), F32),
                        pltpu.VMEM((ts, d), F32),
                        pltpu.VMEM((FFN_HALO * SUBLANES, 2 * d_ff), F32),
                        pltpu.VMEM((ts, d_ff), BF16),
                        *[pltpu.VMEM(s, BF16) for s in matmul_shapes],
                        pltpu.VMEM((CONV_WIDTH, SUBLANES, d_conv), F32),
                        pltpu.VMEM((STAGE_SLOTS, STAGE_GROUPS, SUBLANES, STAGE_COLS), F32),
                        pltpu.SemaphoreType.DMA((STAGE_SLOTS + 2 * N_SLOTS - 1,))],
        compiler_params=_params(1),
        name="layer",
    )(x, k, v, *weights)


def kernel(x, mem, norm_mix_g, w_in, conv_dw_w, conv_dw_b, conv_ln_g, conv_ln_b,
           pool_w, pool_scale, w_out, norm_xattn_g, norm_mem_g, w_q, w_kv, w_o,
           norm_ffn_g, w_up, ffn_dw_w, ffn_dw_b, w_down, norm_final_g):
    depth = w_in.shape[0]
    d = x.shape[-1]
    chunk_len = SEQ_TILE // SUBLANES
    assert x.shape[1] % SEQ_TILE == 0 and SEQ_TILE % CONV_ROWS == 0
    assert x.shape[0] * (x.shape[1] // SEQ_TILE) >= N_SLOTS
    assert conv_dw_w.shape[1] == CONV_WIDTH and ffn_dw_w.shape[1] == FFN_CONV_WIDTH
    assert CONV_WIDTH - 1 <= CONV_HALO <= chunk_len
    assert max(POOL_WINDOWS) - 1 <= POOL_HALO <= chunk_len
    assert FFN_CONV_WIDTH - 1 <= FFN_HALO <= chunk_len
    scale = float(d // XATTN_HEADS) ** -0.5
    assert math.frexp(scale)[0] == 0.5

    def row(v):
        return v.reshape(1, -1)

    gf = row(norm_final_g)
    for l in range(depth):
        k, v = _kv_call(mem, row(norm_mem_g[l]), w_kv[l], scale)
        mixer_w = (row(norm_mix_g[l]), _row_groups(w_in[l]), conv_dw_w[l],
                   row(conv_dw_b[l]), row(conv_ln_g[l]), row(conv_ln_b[l]),
                   pool_w[l], row(pool_scale[l]), _row_groups(w_out[l]))
        attn_w = (row(norm_xattn_g[l]), _row_groups(w_q[l]), _row_groups(w_o[l]))
        ffn_w = (row(norm_ffn_g[l]), _row_groups(w_up[l]), ffn_dw_w[l],
                 row(ffn_dw_b[l]), _row_groups(w_down[l]), gf)
        x = _layer_call(x, k, v, mixer_w, attn_w, ffn_w, final_norm=(l == depth - 1))
    return x
```
